```python
import math, functools
import jax, jax.numpy as jnp
from jax import lax
import numpy as np

D_MODEL = 1024
BATCH = 16
SEQ = 2048
DEPTH = 4
DEC_BATCH = 128
DEC_SEQ = 8
PAST_LEN = 8192
PAGE_SIZE = 128

F32 = jnp.float32
EPS = 1e-6
POOL_WINDOWS = (2, 4, 8, 16)
POOL_GROUPS = len(POOL_WINDOWS)
POOL_WIDTH = D_MODEL
POOL_GW = POOL_WIDTH // POOL_GROUPS
POOL_STATE = max(POOL_WINDOWS) - 1
GLA_HEADS = 4
GLA_DK = D_MODEL // (2 * GLA_HEADS)
GLA_DV = D_MODEL // GLA_HEADS
GLA_RANK = 16
GLA_TAU = 16.0
GLA_CHUNK = 64
MLA_HEADS = 8
MLA_NOPE = 128
MLA_ROPE = 64
MLA_DQK = MLA_NOPE + MLA_ROPE
MLA_DV = D_MODEL // MLA_HEADS
Q_LORA = D_MODEL // 4
KV_LORA = D_MODEL // 4
ROPE_THETA = 10000.0
Q_BLOCK = 128
ATTN_SCALE = MLA_DQK ** -0.5
N_BRANCH = 3
D_FF = 4 * D_MODEL
IN_SPLITS = (POOL_WIDTH, GLA_HEADS * GLA_DK, GLA_HEADS * GLA_DK, GLA_HEADS * GLA_DV, GLA_RANK, D_MODEL,
             Q_LORA, KV_LORA, MLA_ROPE, N_BRANCH * D_MODEL)
IN_COLS = sum(IN_SPLITS)
SPLIT_POINTS = tuple(sum(IN_SPLITS[:i + 1]) for i in range(len(IN_SPLITS) - 1))

kernel_name = "hybrid_pool_gla_mla_decoder_step"


def rms_norm(x, w):
    xf = x.astype(F32)
    y = xf * lax.rsqrt(jnp.mean(xf * xf, axis=-1, keepdims=True) + EPS)
    return (y * w.astype(F32)).astype(x.dtype)


def rope_tables(pos):
    inv = 1.0 / (ROPE_THETA ** (jnp.arange(0, MLA_ROPE, 2, dtype=F32) / MLA_ROPE))
    ang = pos.astype(F32)[:, None] * inv[None, :]
    return jnp.cos(ang)[:, None, :], jnp.sin(ang)[:, None, :]


def apply_rope(x, cos, sin):
    half = x.shape[-1] // 2
    x1 = x[..., :half].astype(F32)
    x2 = x[..., half:].astype(F32)
    return jnp.concatenate([x1 * cos - x2 * sin, x2 * cos + x1 * sin], axis=-1).astype(x.dtype)


def pool_mixer(u, prefix, start_pos, w_pool, pool_scale):
    B, L, _ = u.shape
    ext = jnp.concatenate([prefix.astype(u.dtype), u], axis=1)
    csum = jnp.cumsum(ext.astype(F32), axis=1)
    csum = jnp.concatenate([jnp.zeros((B, 1, POOL_WIDTH), F32), csum], axis=1)
    pos = start_pos + jnp.arange(L)
    parts = []
    for g, w in enumerate(POOL_WINDOWS):
        sl = slice(g * POOL_GW, (g + 1) * POOL_GW)
        lo = POOL_STATE + 1 - w
        wsum = csum[:, POOL_STATE + 1:, sl] - csum[:, lo:lo + L, sl]
        cnt = jnp.minimum(pos + 1, w).astype(F32)[None, :, None]
        parts.append(wsum / cnt - u[:, :, sl].astype(F32))
    pooled = jnp.stack(parts, axis=2).astype(u.dtype)
    y = jnp.einsum('blgc,gcd->blgd', pooled, w_pool).reshape(B, L, POOL_GROUPS * POOL_GW)
    return y * pool_scale, ext[:, -POOL_STATE:]


def gla_mixer(q, k, v, a_low, r, state0, w_a2, b_a, norm_w):
    B, L, _ = q.shape
    C = math.gcd(L, GLA_CHUNK)
    N = L // C
    log_a = jax.nn.log_sigmoid((a_low @ w_a2 + b_a).astype(F32)) / GLA_TAU

    def heads(t, d):
        return t.astype(F32).reshape(B, N, C, GLA_HEADS, d).transpose(0, 3, 1, 2, 4)

    qh = heads(q, GLA_DK) * (GLA_DK ** -0.5)
    kh = heads(k, GLA_DK)
    vh = heads(v, GLA_DV)
    b = jnp.cumsum(heads(log_a, GLA_DK), axis=3)
    b_last = b[:, :, :, -1:]
    q_t = qh * jnp.exp(b)
    k_t = kh * jnp.exp(-b)
    mask = jnp.tril(jnp.ones((C, C), dtype=bool))
    att = jnp.where(mask, jnp.einsum('bhnik,bhnjk->bhnij', q_t, k_t), 0.0)
    o_intra = jnp.einsum('bhnij,bhnjv->bhniv', att, vh)
    kv_upd = jnp.einsum('bhnck,bhncv->nbhkv', kh * jnp.exp(b_last - b), vh)
    decay = jnp.exp(b_last[:, :, :, 0]).transpose(2, 0, 1, 3)

    def step(S, inp):
        d, upd = inp
        return S * d[..., None] + upd, S

    S_final, S_prev = lax.scan(step, state0.astype(F32), (decay, kv_upd))
    o_inter = jnp.einsum('bhnck,nbhkv->bhncv', q_t, S_prev)
    o = (o_intra + o_inter).transpose(0, 2, 3, 1, 4).reshape(B, L, GLA_HEADS, GLA_DV)
    o = rms_norm(o, norm_w).reshape(B, L, GLA_HEADS * GLA_DV).astype(q.dtype)
    return o * jax.nn.silu(r), S_final.astype(state0.dtype)


def mla_keys(c_kv, k_pe, cos, sin, w_uk, k_norm_w):
    k_nope = (c_kv @ w_uk).reshape(c_kv.shape[:-1] + (MLA_HEADS, MLA_NOPE))
    k_rot = jnp.broadcast_to(k_pe[..., None, :], k_nope.shape[:-1] + (MLA_ROPE,))
    k = rms_norm(jnp.concatenate([k_nope, k_rot], axis=-1), k_norm_w)
    return jnp.concatenate([k[..., :MLA_NOPE], apply_rope(k[..., MLA_NOPE:], cos, sin)], axis=-1)


def attend_prompt(q, c_kv, k_pe, w_uk, w_uv, k_norm_w):
    B, L = q.shape[:2]
    cos, sin = rope_tables(jnp.arange(L))
    k = mla_keys(c_kv, k_pe, cos, sin, w_uk, k_norm_w)
    v = (c_kv @ w_uv).reshape(B, L, MLA_HEADS, MLA_DV)
    qb = math.gcd(L, Q_BLOCK)
    nb = L // qb
    q_blocks = q.reshape(B, nb, qb, MLA_HEADS, MLA_DQK).transpose(1, 0, 2, 3, 4)
    kpos = jnp.arange(L)

    def block(args):
        q_i, i = args
        s = jnp.einsum('bqhd,bkhd->bhqk', q_i, k).astype(F32) * ATTN_SCALE
        qpos = i * qb + jnp.arange(qb)
        s = jnp.where(kpos[None, :] <= qpos[:, None], s, -jnp.inf)
        p = jax.nn.softmax(s, axis=-1).astype(v.dtype)
        return jnp.einsum('bhqk,bkhd->bqhd', p, v)

    o = lax.map(block, (q_blocks, jnp.arange(nb)))
    return o.transpose(1, 0, 2, 3, 4).reshape(B, L, MLA_HEADS * MLA_DV)


def attend_sample(q, c_new, pe_new, cache_lat, cache_pe, page_table, layer, w_uk, w_uv, k_norm_w):
    Ld = q.shape[1]
    past = page_table.shape[1] * PAGE_SIZE
    T = past + Ld
    cos, sin = rope_tables(jnp.arange(T))
    kpos = jnp.arange(T)
    qpos = past + jnp.arange(Ld)
    mask = kpos[None, :] <= qpos[:, None]
    w_uv_h = w_uv.reshape(KV_LORA, MLA_HEADS, MLA_DV)

    def one(args):
        qs, cn, pn, row = args
        c_all = jnp.concatenate([cache_lat[layer, row].reshape(past, KV_LORA).astype(cn.dtype), cn], axis=0)
        pe_all = jnp.concatenate([cache_pe[layer, row].reshape(past, MLA_ROPE).astype(pn.dtype), pn], axis=0)
        k = mla_keys(c_all, pe_all, cos, sin, w_uk, k_norm_w)
        s = jnp.einsum('qhd,khd->hqk', qs, k).astype(F32) * ATTN_SCALE
        s = jnp.where(mask[None], s, -jnp.inf)
        p = jax.nn.softmax(s, axis=-1).astype(c_all.dtype)
        o_lat = jnp.einsum('hqk,kc->qhc', p, c_all)
        return jnp.einsum('qhc,chd->qhd', o_lat, w_uv_h).reshape(Ld, MLA_HEADS * MLA_DV)

    return lax.map(one, (q, c_new, pe_new, page_table))


def token_mixers(h, start_pos, pool_prefix, gla_state, attend, w_in, b_gate, w_pool, pool_scale,
                 w_gla_a2, b_gla_a, gla_norm_w, q_a_norm_w, kv_a_norm_w, w_uq, q_norm_w, w_out):
    B, L, _ = h.shape
    u, gq, gk, gv, ga, gr, mq, mkv, mpe, gl = jnp.split(h @ w_in, SPLIT_POINTS, axis=-1)
    y_pool, new_pool = pool_mixer(u, pool_prefix, start_pos, w_pool, pool_scale)
    y_gla, new_gla = gla_mixer(gq, gk, gv, ga, gr, gla_state, w_gla_a2, b_gla_a, gla_norm_w)
    cos, sin = rope_tables(start_pos + jnp.arange(L))
    q = (rms_norm(mq, q_a_norm_w) @ w_uq).reshape(B, L, MLA_HEADS, MLA_DQK)
    q = rms_norm(q, q_norm_w)
    q = jnp.concatenate([q[..., :MLA_NOPE], apply_rope(q[..., MLA_NOPE:], cos, sin)], axis=-1)
    c_kv = rms_norm(mkv, kv_a_norm_w)
    y_mla = attend(q, c_kv, mpe)
    g = jax.nn.sigmoid(gl.reshape(B, L, N_BRANCH, D_MODEL).astype(F32) + b_gate.astype(F32)).astype(h.dtype)
    merged = g[:, :, 0] * y_pool + g[:, :, 1] * y_gla + g[:, :, 2] * y_mla
    return merged @ w_out, new_pool, new_gla, c_kv, mpe


def channel_mlp(x, norm_w, w_up, w_down):
    h = rms_norm(x, norm_w)
    return jnp.square(jax.nn.relu(h @ w_up)) @ w_down


def setup_inputs(seed: int = 0) -> dict:
    key = jax.random.key(seed)
    ks = jax.random.split(key, 32)
    n_pages = PAST_LEN // PAGE_SIZE
    n_phys = (DEC_BATCH * n_pages * 5) // 4

    def nrm(k, shape, scale=1.0):
        return jax.random.normal(k, shape, F32) * scale

    def gain(k, shape):
        return 1.0 + 0.1 * jax.random.normal(k, shape, F32)

    page_table = jax.random.permutation(ks[6], n_phys)[:DEC_BATCH * n_pages]
    page_table = page_table.reshape(DEC_BATCH, n_pages).astype(jnp.int32)
    return {
        "x_prompt": nrm(ks[0], (BATCH, SEQ, D_MODEL)),
        "x_sample": nrm(ks[1], (DEC_BATCH, DEC_SEQ, D_MODEL)),
        "state_pool": nrm(ks[2], (DEPTH, DEC_BATCH, POOL_STATE, POOL_WIDTH)),
        "state_gla": nrm(ks[3], (DEPTH, DEC_BATCH, GLA_HEADS, GLA_DK, GLA_DV)),
        "cache_kv_latent": nrm(ks[4], (DEPTH, n_phys, PAGE_SIZE, KV_LORA)),
        "cache_k_rope": nrm(ks[5], (DEPTH, n_phys, PAGE_SIZE, MLA_ROPE)),
        "page_table": page_table,
        "norm1_w": gain(ks[7], (DEPTH, D_MODEL)),
        "w_in": nrm(ks[8], (DEPTH, D_MODEL, IN_COLS), D_MODEL ** -0.5),
        "b_gate": nrm(ks[9], (DEPTH, N_BRANCH, D_MODEL), 0.1),
        "w_pool": nrm(ks[10], (DEPTH, POOL_GROUPS, POOL_GW, POOL_GW), POOL_GW ** -0.5),
        "pool_scale": gain(ks[11], (DEPTH, D_MODEL)),
        "w_gla_a2": nrm(ks[12], (DEPTH, GLA_RANK, GLA_HEADS * GLA_DK), GLA_RANK ** -0.5),
        "b_gla_a": nrm(ks[13], (DEPTH, GLA_HEADS * GLA_DK), 0.1),
        "gla_norm_w": gain(ks[14], (DEPTH, GLA_DV)),
        "q_a_norm_w": gain(ks[15], (DEPTH, Q_LORA)),
        "kv_a_norm_w": gain(ks[16], (DEPTH, KV_LORA)),
        "w_uq": nrm(ks[17], (DEPTH, Q_LORA, MLA_HEADS * MLA_DQK), Q_LORA ** -0.5),
        "q_norm_w": gain(ks[18], (DEPTH, MLA_DQK)),
        "w_uk": nrm(ks[19], (DEPTH, KV_LORA, MLA_HEADS * MLA_NOPE), KV_LORA ** -0.5),
        "w_uv": nrm(ks[20], (DEPTH, KV_LORA, MLA_HEADS * MLA_DV), KV_LORA ** -0.5),
        "k_norm_w": gain(ks[21], (DEPTH, MLA_DQK)),
        "w_out": nrm(ks[22], (DEPTH, D_MODEL, D_MODEL), D_MODEL ** -0.5),
        "norm2_w": gain(ks[23], (DEPTH, D_MODEL)),
        "w_up": nrm(ks[24], (DEPTH, D_MODEL, D_FF), D_MODEL ** -0.5),
        "w_down": nrm(ks[25], (DEPTH, D_FF, D_MODEL), D_FF ** -0.5),
    }


def reference(x_prompt, x_sample, state_pool, state_gla, cache_kv_latent, cache_k_rope, page_table,
              norm1_w, w_in, b_gate, w_pool, pool_scale, w_gla_a2, b_gla_a, gla_norm_w,
              q_a_norm_w, kv_a_norm_w, w_uq, q_norm_w, w_uk, w_uv, k_norm_w, w_out,
              norm2_w, w_up, w_down):
    xp, xs = x_prompt, x_sample
    B = xp.shape[0]
    pool_p, pool_s, gla_p, gla_s, lat_p, lat_s, pe_p, pe_s = [], [], [], [], [], [], [], []
    for l in range(DEPTH):
        shared = (w_in[l], b_gate[l], w_pool[l], pool_scale[l], w_gla_a2[l], b_gla_a[l], gla_norm_w[l],
                  q_a_norm_w[l], kv_a_norm_w[l], w_uq[l], q_norm_w[l], w_out[l])
        attend_p = functools.partial(attend_prompt, w_uk=w_uk[l], w_uv=w_uv[l], k_norm_w=k_norm_w[l])
        mix_p, np_pool, np_gla, np_lat, np_pe = token_mixers(
            rms_norm(xp, norm1_w[l]), 0, jnp.zeros((B, POOL_STATE, POOL_WIDTH), xp.dtype),
            jnp.zeros((B, GLA_HEADS, GLA_DK, GLA_DV), xp.dtype), attend_p, *shared)
        xp = xp + mix_p
        xp = xp + channel_mlp(xp, norm2_w[l], w_up[l], w_down[l])
        attend_s = functools.partial(attend_sample, cache_lat=cache_kv_latent, cache_pe=cache_k_rope,
                                     page_table=page_table, layer=l, w_uk=w_uk[l], w_uv=w_uv[l],
                                     k_norm_w=k_norm_w[l])
        mix_s, ns_pool, ns_gla, ns_lat, ns_pe = token_mixers(
            rms_norm(xs, norm1_w[l]), PAST_LEN, state_pool[l], state_gla[l], attend_s, *shared)
        xs = xs + mix_s
        xs = xs + channel_mlp(xs, norm2_w[l], w_up[l], w_down[l])
        pool_p.append(np_pool); pool_s.append(ns_pool)
        gla_p.append(np_gla); gla_s.append(ns_gla)
        lat_p.append(np_lat); lat_s.append(ns_lat)
        pe_p.append(np_pe); pe_s.append(ns_pe)
    return (xp, xs, jnp.stack(pool_p), jnp.stack(pool_s), jnp.stack(gla_p), jnp.stack(gla_s),
            jnp.stack(lat_p), jnp.stack(lat_s), jnp.stack(pe_p), jnp.stack(pe_s))
```

```python
import functools
import math

import jax
import jax.numpy as jnp
from jax import lax
from jax.experimental import pallas as pl
from jax.experimental.pallas import tpu as pltpu

F32 = jnp.float32
BF16 = jnp.bfloat16
EPS = 1e-6

D = 1024
POOL_WINDOWS = (2, 4, 8, 16)
POOL_GW = 256
POOL_STATE = 15
GLA_H = 4
GLA_DK = 128
GLA_DV = 256
GLA_RANK = 16
GLA_TAU = 16.0
GLA_CHUNK = 64
MLA_H = 8
MLA_NOPE = 128
MLA_ROPE = 64
MLA_DQK = MLA_NOPE + MLA_ROPE
MLA_DV = 128
Q_LORA = 256
KV_LORA = 256
ROPE_THETA = 10000.0
ATTN_SCALE = MLA_DQK ** -0.5
D_FF = 4 * D
PAGE = 128
HEAD_PAD = 256
LANES = 128

MAIN_COLS = 7680
TAIL_COLS = 128

NT_DIMS = (((1,), (1,)), ((), ()))
TN_DIMS = (((0,), (0,)), ((), ()))


def _cparams(sem, vmem_mb=48):
    return pltpu.CompilerParams(dimension_semantics=sem, vmem_limit_bytes=vmem_mb * 1024 * 1024)


def _dot(a, b):
    return jnp.dot(a, b, preferred_element_type=F32)


def _dot_nt(a, b):
    return lax.dot_general(a, b, NT_DIMS, preferred_element_type=F32)


def _dot_tn(a, b):
    return lax.dot_general(a, b, TN_DIMS, preferred_element_type=F32)


def _rms(x, w):
    return x * lax.rsqrt(jnp.mean(x * x, axis=-1, keepdims=True) + EPS) * w


def _pick(n, target, mult=16):
    best = None
    for t in range(mult, min(n, target) + 1, mult):
        if n % t == 0:
            best = t
    assert best is not None, (n, target)
    return best


def _proj_kernel(x_ref, nw_ref, w_ref, wt_ref, o_ref, ot_ref, h_scr):
    @pl.when(pl.program_id(1) == 0)
    def _():
        h = _rms(x_ref[...], nw_ref[...]).astype(BF16)
        h_scr[...] = h
        ot_ref[...] = _dot(h, wt_ref[...])

    o_ref[...] = _dot(h_scr[...], w_ref[...])


def _proj(x, nw, w_main, w_tail):
    M = x.shape[0]
    tm = _pick(M, 1024)
    tn = 512
    return pl.pallas_call(
        _proj_kernel,
        grid=(M // tm, MAIN_COLS // tn),
        in_specs=[pl.BlockSpec((tm, D), lambda i, j: (i, 0)),
                  pl.BlockSpec((1, D), lambda i, j: (0, 0)),
                  pl.BlockSpec((D, tn), lambda i, j: (0, j)),
                  pl.BlockSpec((D, TAIL_COLS), lambda i, j: (0, 0))],
        out_specs=[pl.BlockSpec((tm, tn), lambda i, j: (i, j)),
                   pl.BlockSpec((tm, TAIL_COLS), lambda i, j: (i, 0))],
        out_shape=[jax.ShapeDtypeStruct((M, MAIN_COLS), F32),
                   jax.ShapeDtypeStruct((M, TAIL_COLS), F32)],
        scratch_shapes=[pltpu.VMEM((tm, D), BF16)],
        compiler_params=_cparams(("parallel", "arbitrary")),
        name="proj",
    )(x, nw, w_main, w_tail)


def _pool_kernel(u_ref, pre_ref, w_ref, sc_ref, y_ref, st_ref, ext_scr, pooled_scr, *, SB, T, NT, start_pos):
    t = pl.program_id(1)
    pos = start_pos + t * T + lax.broadcasted_iota(jnp.int32, (T, 1), 0)
    for s in range(SB):
        @pl.when(t == 0)
        def _():
            ext_scr[s, 1:16, :] = pre_ref[s]

        ext_scr[s, 16:16 + T, :] = u_ref[s * T:(s + 1) * T, :]
        for g, w in enumerate(POOL_WINDOWS):
            cols = slice(g * POOL_GW, (g + 1) * POOL_GW)
            u = ext_scr[s, 16:16 + T, cols]
            acc = u
            for j in range(1, w):
                acc = acc + ext_scr[s, 16 - j:16 - j + T, cols]
            cnt = jnp.minimum(pos + 1, w).astype(F32)
            pooled_scr[s * T:(s + 1) * T, cols] = (acc / cnt - u).astype(BF16)
        last = ext_scr[s, T + 1:T + 16, :]

        @pl.when(t == NT - 1)
        def _():
            st_ref[s] = last

        ext_scr[s, 1:16, :] = last
    for g in range(len(POOL_WINDOWS)):
        cols = slice(g * POOL_GW, (g + 1) * POOL_GW)
        y_ref[:, cols] = _dot(pooled_scr[:, cols], w_ref[g]) * sc_ref[:, cols]


def _pool(p1, prefix, w_pool, scale, *, B, L, start_pos):
    if L >= 512:
        SB, T = 1, 512
    else:
        SB, T = _pick(B, 16, 1), L
    NT = L // T
    kern = functools.partial(_pool_kernel, SB=SB, T=T, NT=NT, start_pos=start_pos)
    return pl.pallas_call(
        kern,
        grid=(B // SB, NT),
        in_specs=[pl.BlockSpec((SB * T, D), lambda g, t: (g * NT + t, 0)),
                  pl.BlockSpec((SB, POOL_STATE, D), lambda g, t: (g, 0, 0)),
                  pl.BlockSpec((4, POOL_GW, POOL_GW), lambda g, t: (0, 0, 0)),
                  pl.BlockSpec((1, D), lambda g, t: (0, 0))],
        out_specs=[pl.BlockSpec((SB * T, D), lambda g, t: (g * NT + t, 0)),
                   pl.BlockSpec((SB, POOL_STATE, D), lambda g, t: (g, 0, 0))],
        out_shape=[jax.ShapeDtypeStruct((B * L, D), F32),
                   jax.ShapeDtypeStruct((B, POOL_STATE, D), F32)],
        scratch_shapes=[pltpu.VMEM((SB, T + 16, D), F32), pltpu.VMEM((SB * T, D), BF16)],
        compiler_params=_cparams(("parallel", "arbitrary")),
        name="pool",
    )(p1, prefix, w_pool, scale)


def _gla_kernel(q_ref, k_ref, v_ref, r_ref, tail_ref, s0_ref, wa_ref, ba_ref, nw_ref,
                o_ref, sout_ref, st_scr, la_scr, *, SB, T, C, NT):
    t = pl.program_id(1)
    z = _dot(tail_ref[...].astype(BF16), wa_ref[...]) + ba_ref[...]
    la_scr[...] = (jnp.minimum(z, 0.0) - jnp.log1p(jnp.exp(-jnp.abs(z)))) * (1.0 / GLA_TAU)

    @pl.when(t == 0)
    def _():
        for s in range(SB):
            for h in range(GLA_H):
                st_scr[s, h] = s0_ref[s, h].T

    row = lax.broadcasted_iota(jnp.int32, (C, C), 0)
    col = lax.broadcasted_iota(jnp.int32, (C, C), 1)
    tril = row >= col
    tri = tril.astype(F32).astype(BF16)
    nw = nw_ref[...]

    def chunk(s, c):
        if isinstance(c, int):
            rows = slice(s * T + c * C, s * T + (c + 1) * C)
        else:
            rows = pl.ds(pl.multiple_of(s * T + c * C, C), C)
        for h in range(GLA_H):
            kc = slice(h * GLA_DK, (h + 1) * GLA_DK)
            vc = slice(h * GLA_DV, (h + 1) * GLA_DV)
            la = la_scr[rows, kc]
            la_hi = la.astype(BF16)
            la_lo = (la - la_hi.astype(F32)).astype(BF16)
            b = _dot(tri, la_hi) + _dot(tri, la_lo)
            bl = b[C - 1:C, :]
            q = q_ref[rows, kc]
            k = k_ref[rows, kc]
            vb = v_ref[rows, vc].astype(BF16)
            qt = (q * (GLA_DK ** -0.5) * jnp.exp(b)).astype(BF16)
            kt = (k * jnp.exp(-b)).astype(BF16)
            kd = (k * jnp.exp(bl - b)).astype(BF16)
            att = jnp.where(tril, _dot_nt(qt, kt), 0.0)
            st = st_scr[s, h]
            o = _dot(att.astype(BF16), vb) + _dot_nt(qt, st.astype(BF16))
            st_scr[s, h] = st * jnp.exp(bl) + _dot_tn(vb, kd)
            r = r_ref[rows, vc]
            o_ref[rows, vc] = _rms(o, nw) * (r * (1.0 / (1.0 + jnp.exp(-r))))

    NC = T // C
    for s in range(SB):
        if NC == 1:
            chunk(s, 0)
        else:
            def body(c, carry, s=s):
                chunk(s, c)
                return carry
            lax.fori_loop(0, NC, body, 0)

    @pl.when(t == NT - 1)
    def _():
        for s in range(SB):
            for h in range(GLA_H):
                sout_ref[s, h] = st_scr[s, h].T


def _gla(p1, p2, s0, wa, ba, nw, *, B, L):
    C = math.gcd(L, GLA_CHUNK)
    if L >= 512:
        SB, T = 1, 512
    else:
        SB, T = _pick(B, 8, 1), L
    NT = L // T
    R = SB * T
    kern = functools.partial(_gla_kernel, SB=SB, T=T, C=C, NT=NT)
    return pl.pallas_call(
        kern,
        grid=(B // SB, NT),
        in_specs=[pl.BlockSpec((R, 512), lambda g, t: (g * NT + t, 2)),
                  pl.BlockSpec((R, 512), lambda g, t: (g * NT + t, 3)),
                  pl.BlockSpec((R, D), lambda g, t: (g * NT + t, 2)),
                  pl.BlockSpec((R, D), lambda g, t: (g * NT + t, 3)),
                  pl.BlockSpec((R, TAIL_COLS), lambda g, t: (g * NT + t, 0)),
                  pl.BlockSpec((SB, GLA_H, GLA_DK, GLA_DV), lambda g, t: (g, 0, 0, 0)),
                  pl.BlockSpec((TAIL_COLS, GLA_H * GLA_DK), lambda g, t: (0, 0)),
                  pl.BlockSpec((1, GLA_H * GLA_DK), lambda g, t: (0, 0)),
                  pl.BlockSpec((1, GLA_DV), lambda g, t: (0, 0))],
        out_specs=[pl.BlockSpec((R, D), lambda g, t: (g * NT + t, 0)),
                   pl.BlockSpec((SB, GLA_H, GLA_DK, GLA_DV), lambda g, t: (g, 0, 0, 0))],
        out_shape=[jax.ShapeDtypeStruct((B * L, D), F32),
                   jax.ShapeDtypeStruct((B, GLA_H, GLA_DK, GLA_DV), F32)],
        scratch_shapes=[pltpu.VMEM((SB, GLA_H, GLA_DV, GLA_DK), F32),
                        pltpu.VMEM((R, GLA_H * GLA_DK), F32)],
        compiler_params=_cparams(("parallel", "arbitrary")),
        name="gla",
    )(p1, p1, p1, p1, p2, s0, wa, ba, nw)


def _rope128(x, cos, sin, lo):
    rot = jnp.where(lo, pltpu.roll(x, 96, 1), pltpu.roll(x, 32, 1))
    return x * cos + rot * sin


def _q_kernel(mq_ref, anw_ref, w_ref, qnw_ref, cos_ref, sin_ref, q_ref):
    xn = _rms(mq_ref[...], anw_ref[...]).astype(BF16)
    q = _dot(xn, w_ref[...])
    lo = lax.broadcasted_iota(jnp.int32, (1, LANES), 1) < 32
    cos = cos_ref[...]
    sin = sin_ref[...]
    wn = qnw_ref[:, 0:128]
    wr = qnw_ref[:, 128:256]
    for h in range(MLA_H):
        qn = q[:, h * HEAD_PAD:h * HEAD_PAD + 128]
        qr = q[:, h * HEAD_PAD + 128:(h + 1) * HEAD_PAD]
        ssq = jnp.sum(qn * qn, axis=-1, keepdims=True) + jnp.sum(qr * qr, axis=-1, keepdims=True)
        rinv = lax.rsqrt(ssq / MLA_DQK + EPS)
        qn = qn * rinv * wn
        qr = _rope128(qr * rinv * wr, cos, sin, lo)
        q_ref[:, h * HEAD_PAD:h * HEAD_PAD + 128] = (qn * ATTN_SCALE).astype(BF16)
        q_ref[:, h * HEAD_PAD + 128:(h + 1) * HEAD_PAD] = (qr * ATTN_SCALE).astype(BF16)


def _qpath(p1, anw, w_uq, qnw, cos, sin):
    M = p1.shape[0]
    tr = cos.shape[0]
    tm = _pick(math.gcd(M, tr), 512)
    nper = tr // tm
    return pl.pallas_call(
        _q_kernel,
        grid=(M // tm,),
        in_specs=[pl.BlockSpec((tm, Q_LORA), lambda i: (i, 28)),
                  pl.BlockSpec((1, Q_LORA), lambda i: (0, 0)),
                  pl.BlockSpec((Q_LORA, MLA_H * HEAD_PAD), lambda i: (0, 0)),
                  pl.BlockSpec((1, HEAD_PAD), lambda i: (0, 0)),
                  pl.BlockSpec((tm, LANES), lambda i: (i % nper, 0)),
                  pl.BlockSpec((tm, LANES), lambda i: (i % nper, 0))],
        out_specs=pl.BlockSpec((tm, MLA_H * HEAD_PAD), lambda i: (i, 0)),
        out_shape=jax.ShapeDtypeStruct((M, MLA_H * HEAD_PAD), BF16),
        compiler_params=_cparams(("parallel",)),
        name="qpath",
    )(p1, anw, w_uq, qnw, cos, sin)


def _kv_kernel(mkv_ref, tail_ref, anw_ref, wuk_ref, wuv_ref, kw_ref, cos_ref, sin_ref, c_ref, *kv_refs):
    c = _rms(mkv_ref[...], anw_ref[...])
    c_ref[...] = c
    if not kv_refs:
        return
    k_ref, v_ref = kv_refs
    cb = c.astype(BF16)
    kn = _dot(cb, wuk_ref[...])
    v_ref[...] = _dot(cb, wuv_ref[...]).astype(BF16)
    lane = lax.broadcasted_iota(jnp.int32, (1, LANES), 1)
    pe = jnp.where(lane < MLA_ROPE, tail_ref[...], 0.0)
    ssq_pe = jnp.sum(pe * pe, axis=-1, keepdims=True)
    kr = _rope128(pe * kw_ref[:, 128:256], cos_ref[...], sin_ref[...], lane < 32)
    wn = kw_ref[:, 0:128]
    for h in range(MLA_H):
        knh = kn[:, h * 128:(h + 1) * 128]
        rinv = lax.rsqrt((jnp.sum(knh * knh, axis=-1, keepdims=True) + ssq_pe) / MLA_DQK + EPS)
        k_ref[:, h * HEAD_PAD:h * HEAD_PAD + 128] = (knh * rinv * wn).astype(BF16)
        k_ref[:, h * HEAD_PAD + 128:(h + 1) * HEAD_PAD] = (kr * rinv).astype(BF16)


def _kvprep(p1, p2, anw, wuk, wuv, kw, cos, sin, *, with_kv):
    M = p1.shape[0]
    tr = cos.shape[0]
    tm = _pick(math.gcd(M, tr), 512)
    nper = tr // tm
    out_specs = [pl.BlockSpec((tm, KV_LORA), lambda i: (i, 0))]
    out_shape = [jax.ShapeDtypeStruct((M, KV_LORA), F32)]
    if with_kv:
        out_specs += [pl.BlockSpec((tm, MLA_H * HEAD_PAD), lambda i: (i, 0)),
                      pl.BlockSpec((tm, MLA_H * MLA_DV), lambda i: (i, 0))]
        out_shape += [jax.ShapeDtypeStruct((M, MLA_H * HEAD_PAD), BF16),
                      jax.ShapeDtypeStruct((M, MLA_H * MLA_DV), BF16)]
    return pl.pallas_call(
        _kv_kernel,
        grid=(M // tm,),
        in_specs=[pl.BlockSpec((tm, KV_LORA), lambda i: (i, 29)),
                  pl.BlockSpec((tm, TAIL_COLS), lambda i: (i, 0)),
                  pl.BlockSpec((1, KV_LORA), lambda i: (0, 0)),
                  pl.BlockSpec((KV_LORA, MLA_H * MLA_NOPE), lambda i: (0, 0)),
                  pl.BlockSpec((KV_LORA, MLA_H * MLA_DV), lambda i: (0, 0)),
                  pl.BlockSpec((1, HEAD_PAD), lambda i: (0, 0)),
                  pl.BlockSpec((tm, LANES), lambda i: (i % nper, 0)),
                  pl.BlockSpec((tm, LANES), lambda i: (i % nper, 0))],
        out_specs=out_specs,
        out_shape=out_shape,
        compiler_params=_cparams(("parallel",)),
        name="kvprep",
    )(p1, p2, anw, wuk, wuv, kw, cos, sin)


def _attn_kernel(q_ref, k_ref, v_ref, o_ref, m_scr, l_scr, acc_scr, *, TQ):
    qi = pl.program_id(2)
    q = q_ref[...]
    m_scr[...] = jnp.full((TQ, 1), -jnp.inf, F32)
    l_scr[...] = jnp.zeros((TQ, 1), F32)
    acc_scr[...] = jnp.zeros((TQ, MLA_DV), F32)

    def step(j, masked):
        rows = pl.ds(pl.multiple_of(j * TQ, TQ), TQ)
        s = _dot_nt(q, k_ref[rows, :])
        if masked:
            r = lax.broadcasted_iota(jnp.int32, (TQ, TQ), 0)
            c = lax.broadcasted_iota(jnp.int32, (TQ, TQ), 1)
            s = jnp.where(c <= r, s, -jnp.inf)
        m_old = m_scr[...]
        m_new = jnp.maximum(m_old, jnp.max(s, axis=-1, keepdims=True))
        alpha = jnp.exp(m_old - m_new)
        p = jnp.exp(s - m_new)
        l_scr[...] = l_scr[...] * alpha + jnp.sum(p, axis=-1, keepdims=True)
        acc_scr[...] = acc_scr[...] * alpha + _dot(p.astype(BF16), v_ref[rows, :])
        m_scr[...] = m_new

    def body(j, carry):
        step(j, False)
        return carry

    lax.fori_loop(0, qi, body, 0)
    step(qi, True)
    o_ref[...] = acc_scr[...] / l_scr[...]


def _attn(q, k, v, *, B, L):
    TQ = _pick(L, 256)
    NQ = L // TQ
    kern = functools.partial(_attn_kernel, TQ=TQ)
    return pl.pallas_call(
        kern,
        grid=(B, MLA_H, NQ),
        in_specs=[pl.BlockSpec((TQ, HEAD_PAD), lambda b, h, i: (b * NQ + i, h)),
                  pl.BlockSpec((L, HEAD_PAD), lambda b, h, i: (b, h)),
                  pl.BlockSpec((L, MLA_DV), lambda b, h, i: (b, h))],
        out_specs=pl.BlockSpec((TQ, MLA_DV), lambda b, h, i: (b * NQ + i, h)),
        out_shape=jax.ShapeDtypeStruct((B * L, MLA_H * MLA_DV), F32),
        scratch_shapes=[pltpu.VMEM((TQ, 1), F32), pltpu.VMEM((TQ, 1), F32), pltpu.VMEM((TQ, MLA_DV), F32)],
        compiler_params=_cparams(("parallel", "parallel", "arbitrary")),
        name="attn",
    )(q, k, v)


def _qs_kernel(q_ref, wukT_ref, kw_ref, qlat_ref, a_ref, *, DB, DL):
    q = q_ref[...].astype(F32)
    qn = (q[:, 0:128] * kw_ref[:, 0:128]).astype(BF16)
    qlat_ref[...] = _dot(qn, wukT_ref[...]).reshape(DB, 1, DL, KV_LORA)
    qr = q[:, 128:256]
    wr = kw_ref[:, 128:256]
    lo = lax.broadcasted_iota(jnp.int32, (1, LANES), 1) < 32
    a1 = qr * wr
    a2 = jnp.where(lo, pltpu.roll(qr, 96, 1), -pltpu.roll(qr, 32, 1)) * wr
    a_ref[...] = (a1 + pltpu.roll(a2, 64, 1)).reshape(DB, 1, DL, LANES)


def _qs(q_s, wukT, kw, *, DB, DL):
    kern = functools.partial(_qs_kernel, DB=DB, DL=DL)
    return pl.pallas_call(
        kern,
        grid=(MLA_H,),
        in_specs=[pl.BlockSpec((DB * DL, HEAD_PAD), lambda h: (0, h)),
                  pl.BlockSpec((MLA_NOPE, KV_LORA), lambda h: (h, 0)),
                  pl.BlockSpec((1, HEAD_PAD), lambda h: (0, 0))],
        out_specs=[pl.BlockSpec((DB, 1, DL, KV_LORA), lambda h: (0, h, 0, 0)),
                   pl.BlockSpec((DB, 1, DL, LANES), lambda h: (0, h, 0, 0))],
        out_shape=[jax.ShapeDtypeStruct((DB, MLA_H, DL, KV_LORA), F32),
                   jax.ShapeDtypeStruct((DB, MLA_H, DL, LANES), F32)],
        compiler_params=_cparams(("parallel",)),
        name="qs",
    )(q_s, wukT, kw)


def _decode_kernel(pt_ref, *refs, PPS, NS, DL):
    lat_refs = refs[:PPS]
    pe_refs = refs[PPS:2 * PPS]
    (cs_ref, qlat_ref, a_ref, wukT_ref, cnew_ref, tnew_ref, wuv_ref, y_ref,
     wall_scr, cb_scr, k2_scr, sqh_scr, sql_scr, s_scr, m_scr, l_scr, acc_scr) = refs[2 * PPS:]
    del pt_ref
    b = pl.program_id(0)
    s = pl.program_id(1)
    NR = MLA_H * DL
    TK = PPS * PAGE

    @pl.when((b == 0) & (s == 0))
    def _():
        wall_scr[0:MLA_H * MLA_NOPE, :] = wukT_ref[...]

    @pl.when(s == 0)
    def _():
        wall_scr[MLA_H * MLA_NOPE:, :] = qlat_ref[...].reshape(NR, KV_LORA).astype(BF16)
        m_scr[...] = jnp.full((NR, 1), -jnp.inf, F32)
        l_scr[...] = jnp.zeros((NR, 1), F32)
        acc_scr[...] = jnp.zeros((NR, KV_LORA), F32)

    a = a_ref[...].reshape(NR, LANES).astype(BF16)

    def scores(cb, k2, sqh, sql):
        n = cb.shape[0]
        big = _dot_nt(wall_scr[...], cb)
        kn = big[0:MLA_H * MLA_NOPE, :].reshape(MLA_H, MLA_NOPE, n)
        ones = jnp.ones((MLA_H, sqh.shape[1]), BF16)
        ssq = jnp.sum(kn * kn, axis=1) + _dot_nt(ones, sqh) + _dot_nt(ones, sql)
        rinv = lax.rsqrt(ssq / MLA_DQK + EPS)
        sc = big[MLA_H * MLA_NOPE:, :] + _dot_nt(a, k2)
        return (sc.reshape(MLA_H, DL, n) * rinv[:, None, :]).reshape(NR, n)

    def update(sc, vals):
        m_old = m_scr[...]
        m_new = jnp.maximum(m_old, jnp.max(sc, axis=-1, keepdims=True))
        alpha = jnp.exp(m_old - m_new)
        p = jnp.exp(sc - m_new)
        l_scr[...] = l_scr[...] * alpha + jnp.sum(p, axis=-1, keepdims=True)
        acc_scr[...] = acc_scr[...] * alpha + _dot(p.astype(BF16), vals)
        m_scr[...] = m_new

    @pl.when(s < NS)
    def _():
        for i in range(PPS):
            rows = slice(i * PAGE, (i + 1) * PAGE)
            cb_scr[rows, :] = lat_refs[i][...].astype(BF16)
            pe = pe_refs[i][...]
            k2_scr[rows, :] = (jnp.concatenate([pe, pe], axis=1) * cs_ref[rows, :]).astype(BF16)
            sq = pe * pe
            sqh = sq.astype(BF16)
            sqh_scr[rows, :] = sqh
            sql_scr[rows, :] = (sq - sqh.astype(F32)).astype(BF16)
        TT = 256 if TK % 256 == 0 else PAGE
        for t in range(TK // TT):
            rows = slice(t * TT, (t + 1) * TT)
            s_scr[:, rows] = scores(cb_scr[rows, :], k2_scr[rows, :], sqh_scr[rows, :], sql_scr[rows, :])
        update(s_scr[...], cb_scr[...])

    @pl.when(s == NS)
    def _():
        lane = lax.broadcasted_iota(jnp.int32, (1, LANES), 1)
        cn = jnp.concatenate([cnew_ref[...], jnp.zeros((PAGE - DL, KV_LORA), F32)], axis=0).astype(BF16)
        pe = jnp.where(lane < MLA_ROPE, tnew_ref[...], 0.0)
        pe2 = pe + pltpu.roll(pe, 64, 1)
        k2 = jnp.concatenate([pe2 * cs_ref[0:DL, :], jnp.zeros((PAGE - DL, LANES), F32)], axis=0).astype(BF16)
        sq = jnp.concatenate([pe * pe, jnp.zeros((PAGE - DL, LANES), F32)], axis=0)
        sqh = sq.astype(BF16)
        sql = (sq - sqh.astype(F32)).astype(BF16)
        sc = scores(cn, k2, sqh, sql)
        qidx = lax.broadcasted_iota(jnp.int32, (MLA_H, DL, PAGE), 1).reshape(NR, PAGE)
        kidx = lax.broadcasted_iota(jnp.int32, (NR, PAGE), 1)
        update(jnp.where(kidx <= qidx, sc, -jnp.inf), cn)
        o = (acc_scr[...] / l_scr[...]).astype(BF16)
        full = _dot(o, wuv_ref[...])
        for h in range(MLA_H):
            y_ref[:, h * MLA_DV:(h + 1) * MLA_DV] = full[h * DL:(h + 1) * DL, h * MLA_DV:(h + 1) * MLA_DV]


def _decode(page_table, cache_lat, cache_pe, layer, cs, qlat, a, wukT, c_new, tail_new, wuv, *, DB, DL, PPS):
    n_pages = page_table.shape[1]
    NS = n_pages // PPS
    TK = PPS * PAGE
    NR = MLA_H * DL

    def lat_map(i):
        return lambda b, s, pt: (layer, pt[b, jnp.minimum(s, NS - 1) * PPS + i], 0, 0)

    in_specs = ([pl.BlockSpec((None, None, PAGE, KV_LORA), lat_map(i)) for i in range(PPS)]
                + [pl.BlockSpec((None, None, PAGE, MLA_ROPE), lat_map(i)) for i in range(PPS)]
                + [pl.BlockSpec((TK, LANES), lambda b, s, pt: (s, 0)),
                   pl.BlockSpec((None, MLA_H, DL, KV_LORA), lambda b, s, pt: (b, 0, 0, 0)),
                   pl.BlockSpec((None, MLA_H, DL, LANES), lambda b, s, pt: (b, 0, 0, 0)),
                   pl.BlockSpec((MLA_H * MLA_NOPE, KV_LORA), lambda b, s, pt: (0, 0)),
                   pl.BlockSpec((DL, KV_LORA), lambda b, s, pt: (b, 0)),
                   pl.BlockSpec((DL, TAIL_COLS), lambda b, s, pt: (b, 0)),
                   pl.BlockSpec((KV_LORA, MLA_H * MLA_DV), lambda b, s, pt: (0, 0))])
    grid_spec = pltpu.PrefetchScalarGridSpec(
        num_scalar_prefetch=1,
        grid=(DB, NS + 1),
        in_specs=in_specs,
        out_specs=pl.BlockSpec((DL, MLA_H * MLA_DV), lambda b, s, pt: (b, 0)),
        scratch_shapes=[pltpu.VMEM((MLA_H * MLA_NOPE + NR, KV_LORA), BF16),
                        pltpu.VMEM((TK, KV_LORA), BF16),
                        pltpu.VMEM((TK, LANES), BF16),
                        pltpu.VMEM((TK, MLA_ROPE), BF16),
                        pltpu.VMEM((TK, MLA_ROPE), BF16),
                        pltpu.VMEM((NR, TK), F32),
                        pltpu.VMEM((NR, 1), F32),
                        pltpu.VMEM((NR, 1), F32),
                        pltpu.VMEM((NR, KV_LORA), F32)])
    kern = functools.partial(_decode_kernel, PPS=PPS, NS=NS, DL=DL)
    return pl.pallas_call(
        kern,
        grid_spec=grid_spec,
        out_shape=jax.ShapeDtypeStruct((DB * DL, MLA_H * MLA_DV), F32),
        compiler_params=_cparams(("arbitrary", "arbitrary")),
        name="decode",
    )(page_table, *([cache_lat] * PPS), *([cache_pe] * PPS), cs, qlat, a, wukT, c_new, tail_new, wuv)


def _merge_kernel(g0_ref, g1_ref, g2_ref, bg_ref, yp_ref, yg_ref, ym_ref, x_ref, w_ref, o_ref):
    def gate(g_ref, i):
        return 1.0 / (1.0 + jnp.exp(-(g_ref[...] + bg_ref[i:i + 1, :])))

    merged = gate(g0_ref, 0) * yp_ref[...] + gate(g1_ref, 1) * yg_ref[...] + gate(g2_ref, 2) * ym_ref[...]
    o_ref[...] = x_ref[...] + _dot(merged.astype(BF16), w_ref[...])


def _merge(p1, bg, yp, yg, ym, x, w_out):
    M = x.shape[0]
    tm = _pick(M, 512)
    row = lambda i: (i, 0)
    return pl.pallas_call(
        _merge_kernel,
        grid=(M // tm,),
        in_specs=[pl.BlockSpec((tm, D), lambda i: (i, 4)),
                  pl.BlockSpec((tm, D), lambda i: (i, 5)),
                  pl.BlockSpec((tm, D), lambda i: (i, 6)),
                  pl.BlockSpec((3, D), lambda i: (0, 0)),
                  pl.BlockSpec((tm, D), row), pl.BlockSpec((tm, D), row), pl.BlockSpec((tm, D), row),
                  pl.BlockSpec((tm, D), row),
                  pl.BlockSpec((D, D), lambda i: (0, 0))],
        out_specs=pl.BlockSpec((tm, D), row),
        out_shape=jax.ShapeDtypeStruct((M, D), F32),
        compiler_params=_cparams(("parallel",)),
        name="merge",
    )(p1, p1, p1, bg, yp, yg, ym, x, w_out)


def _mlp_kernel(x_ref, nw_ref, wu_ref, wd_ref, o_ref, h_scr, acc_scr):
    f = pl.program_id(1)

    @pl.when(f == 0)
    def _():
        h_scr[...] = _rms(x_ref[...], nw_ref[...]).astype(BF16)
        acc_scr[...] = jnp.zeros_like(acc_scr)

    a = jnp.maximum(_dot(h_scr[...], wu_ref[...]), 0.0)
    acc_scr[...] += _dot((a * a).astype(BF16), wd_ref[...])

    @pl.when(f == pl.num_programs(1) - 1)
    def _():
        o_ref[...] = x_ref[...] + acc_scr[...]


def _mlp(x, nw, w_up, w_down):
    M = x.shape[0]
    tm = _pick(M, 1024)
    tf = 512
    return pl.pallas_call(
        _mlp_kernel,
        grid=(M // tm, D_FF // tf),
        in_specs=[pl.BlockSpec((tm, D), lambda i, f: (i, 0)),
                  pl.BlockSpec((1, D), lambda i, f: (0, 0)),
                  pl.BlockSpec((D, tf), lambda i, f: (0, f)),
                  pl.BlockSpec((tf, D), lambda i, f: (f, 0))],
        out_specs=pl.BlockSpec((tm, D), lambda i, f: (i, 0)),
        out_shape=jax.ShapeDtypeStruct((M, D), F32),
        scratch_shapes=[pltpu.VMEM((tm, D), BF16), pltpu.VMEM((tm, D), F32)],
        compiler_params=_cparams(("parallel", "arbitrary")),
        name="mlp",
    )(x, nw, w_up, w_down)


def _rope_angles(pos):
    inv = 1.0 / (ROPE_THETA ** (jnp.arange(0, MLA_ROPE, 2, dtype=F32) / MLA_ROPE))
    return pos.astype(F32)[:, None] * inv[None, :]


def _token_tables(pos):
    ang = _rope_angles(pos)
    c, s = jnp.cos(ang), jnp.sin(ang)
    z = jnp.zeros((pos.shape[0], LANES - MLA_ROPE), F32)
    return jnp.concatenate([c, c, z], axis=1), jnp.concatenate([-s, s, z], axis=1)


def _key_table(n):
    ang = _rope_angles(jnp.arange(n))
    c, s = jnp.cos(ang), jnp.sin(ang)
    return jnp.concatenate([c, c, s, s], axis=1)


def _pad_heads(w):
    lead = w.shape[:-1]
    w = w.reshape(lead + (MLA_H, MLA_DQK))
    w = jnp.concatenate([w, jnp.zeros(lead + (MLA_H, HEAD_PAD - MLA_DQK), w.dtype)], axis=-1)
    return w.reshape(lead + (MLA_H * HEAD_PAD,))


def kernel(x_prompt, x_sample, state_pool, state_gla, cache_kv_latent, cache_k_rope, page_table,
           norm1_w, w_in, b_gate, w_pool, pool_scale, w_gla_a2, b_gla_a, gla_norm_w,
           q_a_norm_w, kv_a_norm_w, w_uq, q_norm_w, w_uk, w_uv, k_norm_w, w_out,
           norm2_w, w_up, w_down):
    return _forward(x_prompt, x_sample, state_pool, state_gla, cache_kv_latent, cache_k_rope, page_table,
                    norm1_w, w_in, b_gate, w_pool, pool_scale, w_gla_a2, b_gla_a, gla_norm_w,
                    q_a_norm_w, kv_a_norm_w, w_uq, q_norm_w, w_uk, w_uv, k_norm_w, w_out,
                    norm2_w, w_up, w_down, pages_per_step=8)


def _forward(x_prompt, x_sample, state_pool, state_gla, cache_kv_latent, cache_k_rope, page_table,
             norm1_w, w_in, b_gate, w_pool, pool_scale, w_gla_a2, b_gla_a, gla_norm_w,
             q_a_norm_w, kv_a_norm_w, w_uq, q_norm_w, w_uk, w_uv, k_norm_w, w_out,
             norm2_w, w_up, w_down, *, pages_per_step):
    B, L, _ = x_prompt.shape
    DB, DL, _ = x_sample.shape
    depth = w_in.shape[0]
    n_pages = page_table.shape[1]
    past = n_pages * PAGE
    PPS = pages_per_step
    assert n_pages % PPS == 0

    o = [0]
    for n in (1024, 512, 512, 1024, GLA_RANK, 1024, Q_LORA, KV_LORA, MLA_ROPE, 3 * D):
        o.append(o[-1] + n)
    seg = lambda i: w_in[:, :, o[i]:o[i + 1]]
    w_main = jnp.concatenate([seg(0), seg(1), seg(2), seg(3), seg(5), seg(9), seg(6), seg(7)], axis=-1).astype(BF16)
    w_tail = jnp.concatenate([seg(8), seg(4), jnp.zeros((depth, D, TAIL_COLS - MLA_ROPE - GLA_RANK), F32)],
                             axis=-1).astype(BF16)
    wa_pad = jnp.zeros((depth, TAIL_COLS, GLA_H * GLA_DK), F32).at[:, MLA_ROPE:MLA_ROPE + GLA_RANK, :].set(w_gla_a2)
    wa_pad = wa_pad.astype(BF16)
    w_pool_b = w_pool.astype(BF16)
    w_uq_p = _pad_heads(w_uq).astype(BF16)
    qnw_p = jnp.concatenate([q_norm_w, jnp.zeros((depth, HEAD_PAD - MLA_DQK), F32)], axis=-1)[:, None, :]
    knw_p = jnp.concatenate([k_norm_w, jnp.zeros((depth, HEAD_PAD - MLA_DQK), F32)], axis=-1)[:, None, :]
    w_uk_b = w_uk.astype(BF16)
    w_ukT_b = jnp.swapaxes(w_uk, 1, 2).astype(BF16)
    w_uv_b = w_uv.astype(BF16)
    w_out_b = w_out.astype(BF16)
    w_up_b = w_up.astype(BF16)
    w_down_b = w_down.astype(BF16)

    cos_p, sin_p = _token_tables(jnp.arange(L))
    reps = math.gcd(DB, 512 // DL)
    cos_s, sin_s = _token_tables(jnp.tile(past + jnp.arange(DL), reps))
    cs_keys = _key_table(past + PPS * PAGE)

    xp = x_prompt.reshape(B * L, D)
    xs = x_sample.reshape(DB * DL, D)
    zero_prefix = jnp.zeros((B, POOL_STATE, D), F32)
    zero_state = jnp.zeros((B, GLA_H, GLA_DK, GLA_DV), F32)

    outs = [[] for _ in range(8)]
    for l in range(depth):
        r1 = lambda w: w[l][None, :]
        p1, p2 = _proj(xp, r1(norm1_w), w_main[l], w_tail[l])
        yp, pool_p = _pool(p1, zero_prefix, w_pool_b[l], r1(pool_scale), B=B, L=L, start_pos=0)
        yg, gla_p = _gla(p1, p2, zero_state, wa_pad[l], r1(b_gla_a), r1(gla_norm_w), B=B, L=L)
        q = _qpath(p1, r1(q_a_norm_w), w_uq_p[l], qnw_p[l], cos_p, sin_p)
        c_p, k, v = _kvprep(p1, p2, r1(kv_a_norm_w), w_uk_b[l], w_uv_b[l], knw_p[l], cos_p, sin_p, with_kv=True)
        ym = _attn(q, k, v, B=B, L=L)
        xp = _merge(p1, b_gate[l], yp, yg, ym, xp, w_out_b[l])
        xp = _mlp(xp, r1(norm2_w), w_up_b[l], w_down_b[l])
        pe_p = p2[:, :MLA_ROPE]
        s1, s2 = _proj(xs, r1(norm1_w), w_main[l], w_tail[l])
        yp, pool_s = _pool(s1, state_pool[l], w_pool_b[l], r1(pool_scale), B=DB, L=DL, start_pos=past)
        yg, gla_s = _gla(s1, s2, state_gla[l], wa_pad[l], r1(b_gla_a), r1(gla_norm_w), B=DB, L=DL)
        q = _qpath(s1, r1(q_a_norm_w), w_uq_p[l], qnw_p[l], cos_s, sin_s)
        (c_s,) = _kvprep(s1, s2, r1(kv_a_norm_w), w_uk_b[l], w_uv_b[l], knw_p[l], cos_s, sin_s, with_kv=False)
        qlat, a = _qs(q, w_ukT_b[l], knw_p[l], DB=DB, DL=DL)
        ym = _decode(page_table, cache_kv_latent, cache_k_rope, l, cs_keys, qlat, a, w_ukT_b[l], c_s, s2,
                     w_uv_b[l], DB=DB, DL=DL, PPS=PPS)
        xs = _merge(s1, b_gate[l], yp, yg, ym, xs, w_out_b[l])
        xs = _mlp(xs, r1(norm2_w), w_up_b[l], w_down_b[l])
        pe_s = s2[:, :MLA_ROPE]
        for lst, val in zip(outs, (pool_p, pool_s, gla_p, gla_s,
                                   c_p.reshape(B, L, KV_LORA), c_s.reshape(DB, DL, KV_LORA),
                                   pe_p.reshape(B, L, MLA_ROPE), pe_s.reshape(DB, DL, MLA_ROPE))):
            lst.append(val)
    return (xp.reshape(B, L, D), xs.reshape(DB, DL, D)) + tuple(jnp.stack(v) for v in outs)
```

```python
import functools
import math

import jax
import jax.numpy as jnp
from jax import lax
from jax.experimental import pallas as pl
from jax.experimental.pallas import tpu as pltpu

F32 = jnp.float32
BF16 = jnp.bfloat16
EPS = 1e-6

D = 1024
POOL_WINDOWS = (2, 4, 8, 16)
POOL_GW = 256
POOL_STATE = 15
GLA_H = 4
GLA_DK = 128
GLA_DV = 256
GLA_RANK = 16
GLA_TAU = 16.0
GLA_CHUNK = 64
MLA_H = 8
MLA_NOPE = 128
MLA_ROPE = 64
MLA_DQK = MLA_NOPE + MLA_ROPE
MLA_DV = 128
Q_LORA = 256
KV_LORA = 256
ROPE_THETA = 10000.0
ATTN_SCALE = MLA_DQK ** -0.5
D_FF = 4 * D
PAGE = 128
HEAD_PAD = 256
LANES = 128

MAIN_COLS = 7680
TAIL_COLS = 128

NT_DIMS = (((1,), (1,)), ((), ()))
TN_DIMS = (((0,), (0,)), ((), ()))


def _cparams(sem, vmem_mb=48):
    return pltpu.CompilerParams(dimension_semantics=sem, vmem_limit_bytes=vmem_mb * 1024 * 1024)


def _dot(a, b):
    return jnp.dot(a, b, preferred_element_type=F32)


def _dot_nt(a, b):
    return lax.dot_general(a, b, NT_DIMS, preferred_element_type=F32)


def _dot_tn(a, b):
    return lax.dot_general(a, b, TN_DIMS, preferred_element_type=F32)


def _rms(x, w):
    return x * lax.rsqrt(jnp.mean(x * x, axis=-1, keepdims=True) + EPS) * w


def _pick(n, target, mult=16):
    best = None
    for t in range(mult, min(n, target) + 1, mult):
        if n % t == 0:
            best = t
    assert best is not None, (n, target)
    return best


def _proj_kernel(x_ref, nw_ref, w_ref, wt_ref, o_ref, ot_ref, h_scr):
    @pl.when(pl.program_id(1) == 0)
    def _():
        h = _rms(x_ref[...], nw_ref[...]).astype(BF16)
        h_scr[...] = h
        ot_ref[...] = _dot(h, wt_ref[...])

    o_ref[...] = _dot(h_scr[...], w_ref[...])


def _proj(x, nw, w_main, w_tail):
    M = x.shape[0]
    tm = _pick(M, 1024)
    tn = 512
    return pl.pallas_call(
        _proj_kernel,
        grid=(M // tm, MAIN_COLS // tn),
        in_specs=[pl.BlockSpec((tm, D), lambda i, j: (i, 0)),
                  pl.BlockSpec((1, D), lambda i, j: (0, 0)),
                  pl.BlockSpec((D, tn), lambda i, j: (0, j)),
                  pl.BlockSpec((D, TAIL_COLS), lambda i, j: (0, 0))],
        out_specs=[pl.BlockSpec((tm, tn), lambda i, j: (i, j)),
                   pl.BlockSpec((tm, TAIL_COLS), lambda i, j: (i, 0))],
        out_shape=[jax.ShapeDtypeStruct((M, MAIN_COLS), F32),
                   jax.ShapeDtypeStruct((M, TAIL_COLS), F32)],
        scratch_shapes=[pltpu.VMEM((tm, D), BF16)],
        compiler_params=_cparams(("parallel", "arbitrary")),
        name="proj",
    )(x, nw, w_main, w_tail)


def _pool_kernel(u_ref, pre_ref, w_ref, sc_ref, y_ref, st_ref, ext_scr, pooled_scr, *, SB, T, NT, start_pos):
    t = pl.program_id(1)
    pos = start_pos + t * T + lax.broadcasted_iota(jnp.int32, (T, 1), 0)
    for s in range(SB):
        @pl.when(t == 0)
        def _():
            ext_scr[s, 1:16, :] = pre_ref[s]

        ext_scr[s, 16:16 + T, :] = u_ref[s * T:(s + 1) * T, :]
        for g, w in enumerate(POOL_WINDOWS):
            cols = slice(g * POOL_GW, (g + 1) * POOL_GW)
            u = ext_scr[s, 16:16 + T, cols]
            acc = u
            for j in range(1, w):
                acc = acc + ext_scr[s, 16 - j:16 - j + T, cols]
            cnt = jnp.minimum(pos + 1, w).astype(F32)
            pooled_scr[s * T:(s + 1) * T, cols] = (acc / cnt - u).astype(BF16)
        last = ext_scr[s, T + 1:T + 16, :]

        @pl.when(t == NT - 1)
        def _():
            st_ref[s] = last

        ext_scr[s, 1:16, :] = last
    for g in range(len(POOL_WINDOWS)):
        cols = slice(g * POOL_GW, (g + 1) * POOL_GW)
        y_ref[:, cols] = _dot(pooled_scr[:, cols], w_ref[g]) * sc_ref[:, cols]


def _pool(p1, prefix, w_pool, scale, *, B, L, start_pos):
    if L >= 512:
        SB, T = 1, 512
    else:
        SB, T = _pick(B, 16, 1), L
    NT = L // T
    kern = functools.partial(_pool_kernel, SB=SB, T=T, NT=NT, start_pos=start_pos)
    return pl.pallas_call(
        kern,
        grid=(B // SB, NT),
        in_specs=[pl.BlockSpec((SB * T, D), lambda g, t: (g * NT + t, 0)),
                  pl.BlockSpec((SB, POOL_STATE, D), lambda g, t: (g, 0, 0)),
                  pl.BlockSpec((4, POOL_GW, POOL_GW), lambda g, t: (0, 0, 0)),
                  pl.BlockSpec((1, D), lambda g, t: (0, 0))],
        out_specs=[pl.BlockSpec((SB * T, D), lambda g, t: (g * NT + t, 0)),
                   pl.BlockSpec((SB, POOL_STATE, D), lambda g, t: (g, 0, 0))],
        out_shape=[jax.ShapeDtypeStruct((B * L, D), F32),
                   jax.ShapeDtypeStruct((B, POOL_STATE, D), F32)],
        scratch_shapes=[pltpu.VMEM((SB, T + 16, D), F32), pltpu.VMEM((SB * T, D), BF16)],
        compiler_params=_cparams(("parallel", "arbitrary")),
        name="pool",
    )(p1, prefix, w_pool, scale)


def _gla_kernel(q_ref, k_ref, v_ref, r_ref, tail_ref, s0_ref, wa_ref, ba_ref, nw_ref,
                o_ref, sout_ref, st_scr, la_scr, *, SB, T, C, NT):
    t = pl.program_id(1)
    z = _dot(tail_ref[...].astype(BF16), wa_ref[...]) + ba_ref[...]
    la_scr[...] = (jnp.minimum(z, 0.0) - jnp.log1p(jnp.exp(-jnp.abs(z)))) * (1.0 / GLA_TAU)

    @pl.when(t == 0)
    def _():
        for s in range(SB):
            for h in range(GLA_H):
                st_scr[s, h] = s0_ref[s, h].T

    row = lax.broadcasted_iota(jnp.int32, (C, C), 0)
    col = lax.broadcasted_iota(jnp.int32, (C, C), 1)
    tril = row >= col
    tri = tril.astype(F32).astype(BF16)
    nw = nw_ref[...]

    def chunk(s, c):
        if isinstance(c, int):
            rows = slice(s * T + c * C, s * T + (c + 1) * C)
        else:
            rows = pl.ds(pl.multiple_of(s * T + c * C, C), C)
        for h in range(GLA_H):
            kc = slice(h * GLA_DK, (h + 1) * GLA_DK)
            vc = slice(h * GLA_DV, (h + 1) * GLA_DV)
            la = la_scr[rows, kc]
            la_hi = la.astype(BF16)
            la_lo = (la - la_hi.astype(F32)).astype(BF16)
            b = _dot(tri, la_hi) + _dot(tri, la_lo)
            bl = b[C - 1:C, :]
            q = q_ref[rows, kc]
            k = k_ref[rows, kc]
            vb = v_ref[rows, vc].astype(BF16)
            qt = (q * (GLA_DK ** -0.5) * jnp.exp(b)).astype(BF16)
            kt = (k * jnp.exp(-b)).astype(BF16)
            kd = (k * jnp.exp(bl - b)).astype(BF16)
            att = jnp.where(tril, _dot_nt(qt, kt), 0.0)
            st = st_scr[s, h]
            o = _dot(att.astype(BF16), vb) + _dot_nt(qt, st.astype(BF16))
            st_scr[s, h] = st * jnp.exp(bl) + _dot_tn(vb, kd)
            r = r_ref[rows, vc]
            o_ref[rows, vc] = _rms(o, nw) * (r * (1.0 / (1.0 + jnp.exp(-r))))

    NC = T // C
    for s in range(SB):
        if NC == 1:
            chunk(s, 0)
        else:
            def body(c, carry, s=s):
                chunk(s, c)
                return carry
            lax.fori_loop(0, NC, body, 0)

    @pl.when(t == NT - 1)
    def _():
        for s in range(SB):
            for h in range(GLA_H):
                sout_ref[s, h] = st_scr[s, h].T


def _gla(p1, p2, s0, wa, ba, nw, *, B, L):
    C = math.gcd(L, GLA_CHUNK)
    if L >= 512:
        SB, T = 1, 512
    else:
        SB, T = _pick(B, 8, 1), L
    NT = L // T
    R = SB * T
    kern = functools.partial(_gla_kernel, SB=SB, T=T, C=C, NT=NT)
    return pl.pallas_call(
        kern,
        grid=(B // SB, NT),
        in_specs=[pl.BlockSpec((R, 512), lambda g, t: (g * NT + t, 2)),
                  pl.BlockSpec((R, 512), lambda g, t: (g * NT + t, 3)),
                  pl.BlockSpec((R, D), lambda g, t: (g * NT + t, 2)),
                  pl.BlockSpec((R, D), lambda g, t: (g * NT + t, 3)),
                  pl.BlockSpec((R, TAIL_COLS), lambda g, t: (g * NT + t, 0)),
                  pl.BlockSpec((SB, GLA_H, GLA_DK, GLA_DV), lambda g, t: (g, 0, 0, 0)),
                  pl.BlockSpec((TAIL_COLS, GLA_H * GLA_DK), lambda g, t: (0, 0)),
                  pl.BlockSpec((1, GLA_H * GLA_DK), lambda g, t: (0, 0)),
                  pl.BlockSpec((1, GLA_DV), lambda g, t: (0, 0))],
        out_specs=[pl.BlockSpec((R, D), lambda g, t: (g * NT + t, 0)),
                   pl.BlockSpec((SB, GLA_H, GLA_DK, GLA_DV), lambda g, t: (g, 0, 0, 0))],
        out_shape=[jax.ShapeDtypeStruct((B * L, D), F32),
                   jax.ShapeDtypeStruct((B, GLA_H, GLA_DK, GLA_DV), F32)],
        scratch_shapes=[pltpu.VMEM((SB, GLA_H, GLA_DV, GLA_DK), F32),
                        pltpu.VMEM((R, GLA_H * GLA_DK), F32)],
        compiler_params=_cparams(("parallel", "arbitrary")),
        name="gla",
    )(p1, p1, p1, p1, p2, s0, wa, ba, nw)


def _rope128(x, cos, sin, lo):
    rot = jnp.where(lo, pltpu.roll(x, 96, 1), pltpu.roll(x, 32, 1))
    return x * cos + rot * sin


def _q_kernel(mq_ref, anw_ref, w_ref, qnw_ref, cos_ref, sin_ref, q_ref):
    xn = _rms(mq_ref[...], anw_ref[...]).astype(BF16)
    q = _dot(xn, w_ref[...])
    lo = lax.broadcasted_iota(jnp.int32, (1, LANES), 1) < 32
    cos = cos_ref[...]
    sin = sin_ref[...]
    wn = qnw_ref[:, 0:128]
    wr = qnw_ref[:, 128:256]
    for h in range(MLA_H):
        qn = q[:, h * HEAD_PAD:h * HEAD_PAD + 128]
        qr = q[:, h * HEAD_PAD + 128:(h + 1) * HEAD_PAD]
        ssq = jnp.sum(qn * qn, axis=-1, keepdims=True) + jnp.sum(qr * qr, axis=-1, keepdims=True)
        rinv = lax.rsqrt(ssq / MLA_DQK + EPS)
        qn = qn * rinv * wn
        qr = _rope128(qr * rinv * wr, cos, sin, lo)
        q_ref[:, h * HEAD_PAD:h * HEAD_PAD + 128] = (qn * ATTN_SCALE).astype(BF16)
        q_ref[:, h * HEAD_PAD + 128:(h + 1) * HEAD_PAD] = (qr * ATTN_SCALE).astype(BF16)


def _qpath(p1, anw, w_uq, qnw, cos, sin):
    M = p1.shape[0]
    tr = cos.shape[0]
    tm = _pick(math.gcd(M, tr), 512)
    nper = tr // tm
    return pl.pallas_call(
        _q_kernel,
        grid=(M // tm,),
        in_specs=[pl.BlockSpec((tm, Q_LORA), lambda i: (i, 28)),
                  pl.BlockSpec((1, Q_LORA), lambda i: (0, 0)),
                  pl.BlockSpec((Q_LORA, MLA_H * HEAD_PAD), lambda i: (0, 0)),
                  pl.BlockSpec((1, HEAD_PAD), lambda i: (0, 0)),
                  pl.BlockSpec((tm, LANES), lambda i: (i % nper, 0)),
                  pl.BlockSpec((tm, LANES), lambda i: (i % nper, 0))],
        out_specs=pl.BlockSpec((tm, MLA_H * HEAD_PAD), lambda i: (i, 0)),
        out_shape=jax.ShapeDtypeStruct((M, MLA_H * HEAD_PAD), BF16),
        compiler_params=_cparams(("parallel",)),
        name="qpath",
    )(p1, anw, w_uq, qnw, cos, sin)


def _kv_kernel(mkv_ref, tail_ref, anw_ref, wuk_ref, wuv_ref, kw_ref, cos_ref, sin_ref, c_ref, *kv_refs):
    c = _rms(mkv_ref[...], anw_ref[...])
    c_ref[...] = c
    if not kv_refs:
        return
    k_ref, v_ref = kv_refs
    cb = c.astype(BF16)
    kn = _dot(cb, wuk_ref[...])
    v_ref[...] = _dot(cb, wuv_ref[...]).astype(BF16)
    lane = lax.broadcasted_iota(jnp.int32, (1, LANES), 1)
    pe = jnp.where(lane < MLA_ROPE, tail_ref[...], 0.0)
    ssq_pe = jnp.sum(pe * pe, axis=-1, keepdims=True)
    kr = _rope128(pe * kw_ref[:, 128:256], cos_ref[...], sin_ref[...], lane < 32)
    wn = kw_ref[:, 0:128]
    for h in range(MLA_H):
        knh = kn[:, h * 128:(h + 1) * 128]
        rinv = lax.rsqrt((jnp.sum(knh * knh, axis=-1, keepdims=True) + ssq_pe) / MLA_DQK + EPS)
        k_ref[:, h * HEAD_PAD:h * HEAD_PAD + 128] = (knh * rinv * wn).astype(BF16)
        k_ref[:, h * HEAD_PAD + 128:(h + 1) * HEAD_PAD] = (kr * rinv).astype(BF16)


def _kvprep(p1, p2, anw, wuk, wuv, kw, cos, sin, *, with_kv):
    M = p1.shape[0]
    tr = cos.shape[0]
    tm = _pick(math.gcd(M, tr), 512)
    nper = tr // tm
    out_specs = [pl.BlockSpec((tm, KV_LORA), lambda i: (i, 0))]
    out_shape = [jax.ShapeDtypeStruct((M, KV_LORA), F32)]
    if with_kv:
        out_specs += [pl.BlockSpec((tm, MLA_H * HEAD_PAD), lambda i: (i, 0)),
                      pl.BlockSpec((tm, MLA_H * MLA_DV), lambda i: (i, 0))]
        out_shape += [jax.ShapeDtypeStruct((M, MLA_H * HEAD_PAD), BF16),
                      jax.ShapeDtypeStruct((M, MLA_H * MLA_DV), BF16)]
    return pl.pallas_call(
        _kv_kernel,
        grid=(M // tm,),
        in_specs=[pl.BlockSpec((tm, KV_LORA), lambda i: (i, 29)),
                  pl.BlockSpec((tm, TAIL_COLS), lambda i: (i, 0)),
                  pl.BlockSpec((1, KV_LORA), lambda i: (0, 0)),
                  pl.BlockSpec((KV_LORA, MLA_H * MLA_NOPE), lambda i: (0, 0)),
                  pl.BlockSpec((KV_LORA, MLA_H * MLA_DV), lambda i: (0, 0)),
                  pl.BlockSpec((1, HEAD_PAD), lambda i: (0, 0)),
                  pl.BlockSpec((tm, LANES), lambda i: (i % nper, 0)),
                  pl.BlockSpec((tm, LANES), lambda i: (i % nper, 0))],
        out_specs=out_specs,
        out_shape=out_shape,
        compiler_params=_cparams(("parallel",)),
        name="kvprep",
    )(p1, p2, anw, wuk, wuv, kw, cos, sin)


def _attn_kernel(q_ref, k_ref, v_ref, o_ref, s_scr, m_scr, l_scr, acc_scr, *, TQ, TK):
    R = TQ // TK
    HL = TK // LANES
    nfull = pl.program_id(2) * R

    def fold(x, op):
        y = x[:, 0:LANES]
        for g in range(1, HL):
            y = op(y, x[:, g * LANES:(g + 1) * LANES])
        return y

    def kv_rows(j):
        return pl.ds(pl.multiple_of(j * TK, TK), TK)

    m_scr[...] = jnp.full((TQ, LANES), -jnp.inf, F32)

    def full_a(j, carry):
        s = _dot_nt(q_ref[...], k_ref[kv_rows(j), :])
        s_scr[j] = s
        m_scr[...] = jnp.maximum(m_scr[...], fold(s, jnp.maximum))
        return carry

    lax.fori_loop(0, nfull, full_a, 0)
    tri = lax.broadcasted_iota(jnp.int32, (TK, TK), 1) <= lax.broadcasted_iota(jnp.int32, (TK, TK), 0)
    for d in range(R):
        r0 = d * TK
        j = nfull + d
        s = _dot_nt(q_ref[r0:, :], k_ref[kv_rows(j), :])
        sd = jnp.where(tri, s[0:TK], -jnp.inf)
        s_scr[j, r0:r0 + TK, :] = sd
        m_scr[r0:r0 + TK, :] = jnp.maximum(m_scr[r0:r0 + TK, :], fold(sd, jnp.maximum))
        if r0 + TK < TQ:
            s_scr[j, r0 + TK:, :] = s[TK:]
            m_scr[r0 + TK:, :] = jnp.maximum(m_scr[r0 + TK:, :], fold(s[TK:], jnp.maximum))

    m_scr[...] = jnp.broadcast_to(jnp.max(m_scr[...], axis=-1, keepdims=True), (TQ, LANES))
    l_scr[...] = jnp.zeros((TQ, LANES), F32)
    acc_scr[...] = jnp.zeros((TQ, MLA_DV), F32)

    def probs(s, m):
        return jnp.concatenate([jnp.exp(s[:, g * LANES:(g + 1) * LANES] - m) for g in range(HL)], axis=1)

    def full_b(j, carry):
        p = probs(s_scr[j], m_scr[...])
        l_scr[...] += fold(p, jnp.add)
        acc_scr[...] += _dot(p.astype(BF16), v_ref[kv_rows(j), :])
        return carry

    lax.fori_loop(0, nfull, full_b, 0)
    for d in range(R):
        r0 = d * TK
        j = nfull + d
        p = probs(s_scr[j, r0:, :], m_scr[r0:, :])
        l_scr[r0:, :] += fold(p, jnp.add)
        acc_scr[r0:, :] += _dot(p.astype(BF16), v_ref[kv_rows(j), :])
    o_ref[...] = acc_scr[...] / jnp.sum(l_scr[...], axis=-1, keepdims=True)


def _attn(q, k, v, *, B, L):
    TK = 256
    TQ = _pick(L, 1024, TK)
    NQ = L // TQ
    kern = functools.partial(_attn_kernel, TQ=TQ, TK=TK)
    return pl.pallas_call(
        kern,
        grid=(B, MLA_H, NQ),
        in_specs=[pl.BlockSpec((TQ, HEAD_PAD), lambda b, h, i: (b * NQ + i, h)),
                  pl.BlockSpec((L, HEAD_PAD), lambda b, h, i: (b, h)),
                  pl.BlockSpec((L, MLA_DV), lambda b, h, i: (b, h))],
        out_specs=pl.BlockSpec((TQ, MLA_DV), lambda b, h, i: (b * NQ + i, h)),
        out_shape=jax.ShapeDtypeStruct((B * L, MLA_H * MLA_DV), F32),
        scratch_shapes=[pltpu.VMEM((L // TK, TQ, TK), F32), pltpu.VMEM((TQ, LANES), F32),
                        pltpu.VMEM((TQ, LANES), F32), pltpu.VMEM((TQ, MLA_DV), F32)],
        compiler_params=_cparams(("parallel", "parallel", "arbitrary")),
        name="attn",
    )(q, k, v)


def _qs_kernel(q_ref, wukT_ref, kw_ref, qlat_ref, a_ref, *, DB, DL):
    q = q_ref[...].astype(F32)
    qn = (q[:, 0:128] * kw_ref[:, 0:128]).astype(BF16)
    qlat_ref[...] = _dot(qn, wukT_ref[...]).reshape(DB, 1, DL, KV_LORA)
    qr = q[:, 128:256]
    wr = kw_ref[:, 128:256]
    lo = lax.broadcasted_iota(jnp.int32, (1, LANES), 1) < 32
    a1 = qr * wr
    a2 = jnp.where(lo, pltpu.roll(qr, 96, 1), -pltpu.roll(qr, 32, 1)) * wr
    a_ref[...] = (a1 + pltpu.roll(a2, 64, 1)).reshape(DB, 1, DL, LANES)


def _qs(q_s, wukT, kw, *, DB, DL):
    kern = functools.partial(_qs_kernel, DB=DB, DL=DL)
    return pl.pallas_call(
        kern,
        grid=(MLA_H,),
        in_specs=[pl.BlockSpec((DB * DL, HEAD_PAD), lambda h: (0, h)),
                  pl.BlockSpec((MLA_NOPE, KV_LORA), lambda h: (h, 0)),
                  pl.BlockSpec((1, HEAD_PAD), lambda h: (0, 0))],
        out_specs=[pl.BlockSpec((DB, 1, DL, KV_LORA), lambda h: (0, h, 0, 0)),
                   pl.BlockSpec((DB, 1, DL, LANES), lambda h: (0, h, 0, 0))],
        out_shape=[jax.ShapeDtypeStruct((DB, MLA_H, DL, KV_LORA), F32),
                   jax.ShapeDtypeStruct((DB, MLA_H, DL, LANES), F32)],
        compiler_params=_cparams(("parallel",)),
        name="qs",
    )(q_s, wukT, kw)


def _decode_kernel(pt_ref, *refs, PPS, NS, DL, CP):
    lat_refs = refs[:PPS]
    pe_refs = refs[PPS:2 * PPS]
    (cst_ref, csn_ref, qlat_ref, a_ref, wukT_ref, cnew_ref, tnew_ref, wuv_ref, y_ref,
     wall_scr, cb_scr, k2t_scr, m_scr, l_scr, acc_scr) = refs[2 * PPS:]
    del pt_ref
    b = pl.program_id(0)
    s = pl.program_id(1)
    NR = MLA_H * DL
    CK = CP * PAGE

    @pl.when((b == 0) & (s == 0))
    def _():
        wall_scr[0:MLA_H * MLA_NOPE, :] = wukT_ref[...]

    @pl.when(s == 0)
    def _():
        wall_scr[MLA_H * MLA_NOPE:, :] = qlat_ref[...].reshape(NR, KV_LORA).astype(BF16)
        m_scr[...] = jnp.full((NR, 1), -jnp.inf, F32)
        l_scr[...] = jnp.zeros((NR, 1), F32)
        acc_scr[...] = jnp.zeros((NR, KV_LORA), F32)

    a = a_ref[...].reshape(NR, LANES).astype(BF16)

    def scores(cb, s_rope, ssq_pe):
        n = cb.shape[0]
        big = _dot_nt(wall_scr[...], cb)
        kn = big[0:MLA_H * MLA_NOPE, :].reshape(MLA_H, MLA_NOPE, n)
        rinv = lax.rsqrt((jnp.sum(kn * kn, axis=1) + ssq_pe) / MLA_DQK + EPS)
        sc = big[MLA_H * MLA_NOPE:, :] + s_rope
        return (sc.reshape(MLA_H, DL, n) * rinv[:, None, :]).reshape(NR, n)

    def update(sc, vals):
        m_old = m_scr[...]
        m_new = jnp.maximum(m_old, jnp.max(sc, axis=-1, keepdims=True))
        alpha = jnp.exp(m_old - m_new)
        p = jnp.exp(sc - m_new)
        l_scr[...] = l_scr[...] * alpha + jnp.sum(p, axis=-1, keepdims=True)
        acc_scr[...] = acc_scr[...] * alpha + _dot(p.astype(BF16), vals)
        m_scr[...] = m_new

    @pl.when(s < NS)
    def _():
        chunks = []
        for c in range(PPS // CP):
            sq = []
            for i in range(c * CP, (c + 1) * CP):
                keys = slice(i * PAGE, (i + 1) * PAGE)
                cb_scr[keys, :] = lat_refs[i][...].astype(BF16)
                pet = pe_refs[i][...]
                k2t_scr[0:MLA_ROPE, keys] = (pet * cst_ref[0:MLA_ROPE, keys]).astype(BF16)
                k2t_scr[MLA_ROPE:, keys] = (pet * cst_ref[MLA_ROPE:, keys]).astype(BF16)
                sq.append(jnp.sum(pet * pet, axis=0, keepdims=True))
            ck = slice(c * CK, (c + 1) * CK)
            s_rope = _dot(a, k2t_scr[:, ck])
            chunks.append((scores(cb_scr[ck, :], s_rope, jnp.concatenate(sq, axis=1)), ck))
        for sc, ck in chunks:
            update(sc, cb_scr[ck, :])

    @pl.when(s == NS)
    def _():
        lane = lax.broadcasted_iota(jnp.int32, (1, LANES), 1)
        pad = lambda x: jnp.concatenate([x, jnp.zeros((PAGE - DL, x.shape[1]), F32)], axis=0)
        cn = pad(cnew_ref[...]).astype(BF16)
        pe = jnp.where(lane < MLA_ROPE, tnew_ref[...], 0.0)
        k2 = pad((pe + pltpu.roll(pe, 64, 1)) * csn_ref[...]).astype(BF16)
        sq = pad(pe * pe)
        sqh = sq.astype(BF16)
        sql = (sq - sqh.astype(F32)).astype(BF16)
        ones = jnp.ones((MLA_H, LANES), BF16)
        sc = scores(cn, _dot_nt(a, k2), _dot_nt(ones, sqh) + _dot_nt(ones, sql))
        qidx = lax.broadcasted_iota(jnp.int32, (MLA_H, DL, PAGE), 1).reshape(NR, PAGE)
        kidx = lax.broadcasted_iota(jnp.int32, (NR, PAGE), 1)
        update(jnp.where(kidx <= qidx, sc, -jnp.inf), cn)
        o = (acc_scr[...] / l_scr[...]).astype(BF16)
        full = _dot(o, wuv_ref[...])
        for h in range(MLA_H):
            y_ref[:, h * MLA_DV:(h + 1) * MLA_DV] = full[h * DL:(h + 1) * DL, h * MLA_DV:(h + 1) * MLA_DV]


def _decode(page_table, cache_lat, cache_pet, layer, cst, csn, qlat, a, wukT, c_new, tail_new, wuv, *, DB, DL, PPS):
    n_pages = page_table.shape[1]
    NS = n_pages // PPS
    TK = PPS * PAGE
    NR = MLA_H * DL
    CP = math.gcd(PPS, 4)

    def page_map(i):
        return lambda b, s, pt: (layer, pt[b, jnp.minimum(s, NS - 1) * PPS + i], 0, 0)

    in_specs = ([pl.BlockSpec((None, None, PAGE, KV_LORA), page_map(i)) for i in range(PPS)]
                + [pl.BlockSpec((None, None, MLA_ROPE, PAGE), page_map(i)) for i in range(PPS)]
                + [pl.BlockSpec((LANES, TK), lambda b, s, pt: (0, jnp.minimum(s, NS - 1))),
                   pl.BlockSpec((DL, LANES), lambda b, s, pt: (0, 0)),
                   pl.BlockSpec((None, MLA_H, DL, KV_LORA), lambda b, s, pt: (b, 0, 0, 0)),
                   pl.BlockSpec((None, MLA_H, DL, LANES), lambda b, s, pt: (b, 0, 0, 0)),
                   pl.BlockSpec((MLA_H * MLA_NOPE, KV_LORA), lambda b, s, pt: (0, 0)),
                   pl.BlockSpec((DL, KV_LORA), lambda b, s, pt: (b, 0)),
                   pl.BlockSpec((DL, TAIL_COLS), lambda b, s, pt: (b, 0)),
                   pl.BlockSpec((KV_LORA, MLA_H * MLA_DV), lambda b, s, pt: (0, 0))])
    grid_spec = pltpu.PrefetchScalarGridSpec(
        num_scalar_prefetch=1,
        grid=(DB, NS + 1),
        in_specs=in_specs,
        out_specs=pl.BlockSpec((DL, MLA_H * MLA_DV), lambda b, s, pt: (b, 0)),
        scratch_shapes=[pltpu.VMEM((MLA_H * MLA_NOPE + NR, KV_LORA), BF16),
                        pltpu.VMEM((TK, KV_LORA), BF16),
                        pltpu.VMEM((LANES, TK), BF16),
                        pltpu.VMEM((NR, 1), F32),
                        pltpu.VMEM((NR, 1), F32),
                        pltpu.VMEM((NR, KV_LORA), F32)])
    kern = functools.partial(_decode_kernel, PPS=PPS, NS=NS, DL=DL, CP=CP)
    return pl.pallas_call(
        kern,
        grid_spec=grid_spec,
        out_shape=jax.ShapeDtypeStruct((DB * DL, MLA_H * MLA_DV), F32),
        compiler_params=_cparams(("arbitrary", "arbitrary")),
        name="decode",
    )(page_table, *([cache_lat] * PPS), *([cache_pet] * PPS), cst, csn, qlat, a, wukT, c_new, tail_new, wuv)


def _merge_kernel(g0_ref, g1_ref, g2_ref, bg_ref, yp_ref, yg_ref, ym_ref, x_ref, w_ref, o_ref):
    def gate(g_ref, i):
        return 1.0 / (1.0 + jnp.exp(-(g_ref[...] + bg_ref[i:i + 1, :])))

    merged = gate(g0_ref, 0) * yp_ref[...] + gate(g1_ref, 1) * yg_ref[...] + gate(g2_ref, 2) * ym_ref[...]
    o_ref[...] = x_ref[...] + _dot(merged.astype(BF16), w_ref[...])


def _merge(p1, bg, yp, yg, ym, x, w_out):
    M = x.shape[0]
    tm = _pick(M, 512)
    row = lambda i: (i, 0)
    return pl.pallas_call(
        _merge_kernel,
        grid=(M // tm,),
        in_specs=[pl.BlockSpec((tm, D), lambda i: (i, 4)),
                  pl.BlockSpec((tm, D), lambda i: (i, 5)),
                  pl.BlockSpec((tm, D), lambda i: (i, 6)),
                  pl.BlockSpec((3, D), lambda i: (0, 0)),
                  pl.BlockSpec((tm, D), row), pl.BlockSpec((tm, D), row), pl.BlockSpec((tm, D), row),
                  pl.BlockSpec((tm, D), row),
                  pl.BlockSpec((D, D), lambda i: (0, 0))],
        out_specs=pl.BlockSpec((tm, D), row),
        out_shape=jax.ShapeDtypeStruct((M, D), F32),
        compiler_params=_cparams(("parallel",)),
        name="merge",
    )(p1, p1, p1, bg, yp, yg, ym, x, w_out)


def _mlp_kernel(x_ref, nw_ref, wu_ref, wd_ref, o_ref, h_scr, acc_scr):
    f = pl.program_id(1)

    @pl.when(f == 0)
    def _():
        h_scr[...] = _rms(x_ref[...], nw_ref[...]).astype(BF16)
        acc_scr[...] = jnp.zeros_like(acc_scr)

    a = jnp.maximum(_dot(h_scr[...], wu_ref[...]), 0.0)
    acc_scr[...] += _dot((a * a).astype(BF16), wd_ref[...])

    @pl.when(f == pl.num_programs(1) - 1)
    def _():
        o_ref[...] = x_ref[...] + acc_scr[...]


def _mlp(x, nw, w_up, w_down):
    M = x.shape[0]
    tm = _pick(M, 1024)
    tf = 512
    return pl.pallas_call(
        _mlp_kernel,
        grid=(M // tm, D_FF // tf),
        in_specs=[pl.BlockSpec((tm, D), lambda i, f: (i, 0)),
                  pl.BlockSpec((1, D), lambda i, f: (0, 0)),
                  pl.BlockSpec((D, tf), lambda i, f: (0, f)),
                  pl.BlockSpec((tf, D), lambda i, f: (f, 0))],
        out_specs=pl.BlockSpec((tm, D), lambda i, f: (i, 0)),
        out_shape=jax.ShapeDtypeStruct((M, D), F32),
        scratch_shapes=[pltpu.VMEM((tm, D), BF16), pltpu.VMEM((tm, D), F32)],
        compiler_params=_cparams(("parallel", "arbitrary")),
        name="mlp",
    )(x, nw, w_up, w_down)


def _rope_angles(pos):
    inv = 1.0 / (ROPE_THETA ** (jnp.arange(0, MLA_ROPE, 2, dtype=F32) / MLA_ROPE))
    return pos.astype(F32)[:, None] * inv[None, :]


def _token_tables(pos):
    ang = _rope_angles(pos)
    c, s = jnp.cos(ang), jnp.sin(ang)
    z = jnp.zeros((pos.shape[0], LANES - MLA_ROPE), F32)
    return jnp.concatenate([c, c, z], axis=1), jnp.concatenate([-s, s, z], axis=1)


def _key_table(pos):
    ang = _rope_angles(pos)
    c, s = jnp.cos(ang), jnp.sin(ang)
    return jnp.concatenate([c, c, s, s], axis=1)


def _pad_heads(w):
    lead = w.shape[:-1]
    w = w.reshape(lead + (MLA_H, MLA_DQK))
    w = jnp.concatenate([w, jnp.zeros(lead + (MLA_H, HEAD_PAD - MLA_DQK), w.dtype)], axis=-1)
    return w.reshape(lead + (MLA_H * HEAD_PAD,))


def kernel(x_prompt, x_sample, state_pool, state_gla, cache_kv_latent, cache_k_rope, page_table,
           norm1_w, w_in, b_gate, w_pool, pool_scale, w_gla_a2, b_gla_a, gla_norm_w,
           q_a_norm_w, kv_a_norm_w, w_uq, q_norm_w, w_uk, w_uv, k_norm_w, w_out,
           norm2_w, w_up, w_down):
    return _forward(x_prompt, x_sample, state_pool, state_gla, cache_kv_latent, cache_k_rope, page_table,
                    norm1_w, w_in, b_gate, w_pool, pool_scale, w_gla_a2, b_gla_a, gla_norm_w,
                    q_a_norm_w, kv_a_norm_w, w_uq, q_norm_w, w_uk, w_uv, k_norm_w, w_out,
                    norm2_w, w_up, w_down, pages_per_step=16)


def _forward(x_prompt, x_sample, state_pool, state_gla, cache_kv_latent, cache_k_rope, page_table,
             norm1_w, w_in, b_gate, w_pool, pool_scale, w_gla_a2, b_gla_a, gla_norm_w,
             q_a_norm_w, kv_a_norm_w, w_uq, q_norm_w, w_uk, w_uv, k_norm_w, w_out,
             norm2_w, w_up, w_down, *, pages_per_step):
    B, L, _ = x_prompt.shape
    DB, DL, _ = x_sample.shape
    depth = w_in.shape[0]
    n_pages = page_table.shape[1]
    past = n_pages * PAGE
    PPS = pages_per_step
    assert n_pages % PPS == 0

    o = [0]
    for n in (1024, 512, 512, 1024, GLA_RANK, 1024, Q_LORA, KV_LORA, MLA_ROPE, 3 * D):
        o.append(o[-1] + n)
    seg = lambda i: w_in[:, :, o[i]:o[i + 1]]
    w_main = jnp.concatenate([seg(0), seg(1), seg(2), seg(3), seg(5), seg(9), seg(6), seg(7)], axis=-1).astype(BF16)
    w_tail = jnp.concatenate([seg(8), seg(4), jnp.zeros((depth, D, TAIL_COLS - MLA_ROPE - GLA_RANK), F32)],
                             axis=-1).astype(BF16)
    wa_pad = jnp.zeros((depth, TAIL_COLS, GLA_H * GLA_DK), F32).at[:, MLA_ROPE:MLA_ROPE + GLA_RANK, :].set(w_gla_a2)
    wa_pad = wa_pad.astype(BF16)
    w_pool_b = w_pool.astype(BF16)
    w_uq_p = _pad_heads(w_uq).astype(BF16)
    qnw_p = jnp.concatenate([q_norm_w, jnp.zeros((depth, HEAD_PAD - MLA_DQK), F32)], axis=-1)[:, None, :]
    knw_p = jnp.concatenate([k_norm_w, jnp.zeros((depth, HEAD_PAD - MLA_DQK), F32)], axis=-1)[:, None, :]
    w_uk_b = w_uk.astype(BF16)
    w_ukT_b = jnp.swapaxes(w_uk, 1, 2).astype(BF16)
    w_uv_b = w_uv.astype(BF16)
    w_out_b = w_out.astype(BF16)
    w_up_b = w_up.astype(BF16)
    w_down_b = w_down.astype(BF16)

    cos_p, sin_p = _token_tables(jnp.arange(L))
    reps = math.gcd(DB, 512 // DL)
    cos_s, sin_s = _token_tables(jnp.tile(past + jnp.arange(DL), reps))
    cs_past_t = _key_table(jnp.arange(past)).T
    cs_new = _key_table(past + jnp.arange(DL))
    cache_pet = jnp.swapaxes(cache_k_rope, 2, 3)

    xp = x_prompt.reshape(B * L, D)
    xs = x_sample.reshape(DB * DL, D)
    zero_prefix = jnp.zeros((B, POOL_STATE, D), F32)
    zero_state = jnp.zeros((B, GLA_H, GLA_DK, GLA_DV), F32)

    outs = [[] for _ in range(8)]
    for l in range(depth):
        r1 = lambda w: w[l][None, :]
        p1, p2 = _proj(xp, r1(norm1_w), w_main[l], w_tail[l])
        yp, pool_p = _pool(p1, zero_prefix, w_pool_b[l], r1(pool_scale), B=B, L=L, start_pos=0)
        yg, gla_p = _gla(p1, p2, zero_state, wa_pad[l], r1(b_gla_a), r1(gla_norm_w), B=B, L=L)
        q = _qpath(p1, r1(q_a_norm_w), w_uq_p[l], qnw_p[l], cos_p, sin_p)
        c_p, k, v = _kvprep(p1, p2, r1(kv_a_norm_w), w_uk_b[l], w_uv_b[l], knw_p[l], cos_p, sin_p, with_kv=True)
        ym = _attn(q, k, v, B=B, L=L)
        xp = _merge(p1, b_gate[l], yp, yg, ym, xp, w_out_b[l])
        xp = _mlp(xp, r1(norm2_w), w_up_b[l], w_down_b[l])
        pe_p = p2[:, :MLA_ROPE]
        s1, s2 = _proj(xs, r1(norm1_w), w_main[l], w_tail[l])
        yp, pool_s = _pool(s1, state_pool[l], w_pool_b[l], r1(pool_scale), B=DB, L=DL, start_pos=past)
        yg, gla_s = _gla(s1, s2, state_gla[l], wa_pad[l], r1(b_gla_a), r1(gla_norm_w), B=DB, L=DL)
        q = _qpath(s1, r1(q_a_norm_w), w_uq_p[l], qnw_p[l], cos_s, sin_s)
        (c_s,) = _kvprep(s1, s2, r1(kv_a_norm_w), w_uk_b[l], w_uv_b[l], knw_p[l], cos_s, sin_s, with_kv=False)
        qlat, a = _qs(q, w_ukT_b[l], knw_p[l], DB=DB, DL=DL)
        ym = _decode(page_table, cache_kv_latent, cache_pet, l, cs_past_t, cs_new, qlat, a, w_ukT_b[l], c_s, s2,
                     w_uv_b[l], DB=DB, DL=DL, PPS=PPS)
        xs = _merge(s1, b_gate[l], yp, yg, ym, xs, w_out_b[l])
        xs = _mlp(xs, r1(norm2_w), w_up_b[l], w_down_b[l])
        pe_s = s2[:, :MLA_ROPE]
        for lst, val in zip(outs, (pool_p, pool_s, gla_p, gla_s,
                                   c_p.reshape(B, L, KV_LORA), c_s.reshape(DB, DL, KV_LORA),
                                   pe_p.reshape(B, L, MLA_ROPE), pe_s.reshape(DB, DL, MLA_ROPE))):
            lst.append(val)
    return (xp.reshape(B, L, D), xs.reshape(DB, DL, D)) + tuple(jnp.stack(v) for v in outs)
```

```python
import functools
import math

import jax
import jax.numpy as jnp
from jax import lax
from jax.experimental import pallas as pl
from jax.experimental.pallas import tpu as pltpu

F32 = jnp.float32
BF16 = jnp.bfloat16
EPS = 1e-6

D = 1024
POOL_WINDOWS = (2, 4, 8, 16)
POOL_GW = 256
POOL_STATE = 15
GLA_H = 4
GLA_DK = 128
GLA_DV = 256
GLA_RANK = 16
GLA_TAU = 16.0
GLA_CHUNK = 64
MLA_H = 8
MLA_NOPE = 128
MLA_ROPE = 64
MLA_DQK = MLA_NOPE + MLA_ROPE
MLA_DV = 128
Q_LORA = 256
KV_LORA = 256
ROPE_THETA = 10000.0
ATTN_SCALE = MLA_DQK ** -0.5
D_FF = 4 * D
PAGE = 128
HEAD_PAD = 256
LANES = 128

F32_COLS = 1536
B16_COLS = 6144
TAIL_COLS = 128

NT_DIMS = (((1,), (1,)), ((), ()))
TN_DIMS = (((0,), (0,)), ((), ()))


def _cparams(sem, vmem_mb=48):
    return pltpu.CompilerParams(dimension_semantics=sem, vmem_limit_bytes=vmem_mb * 1024 * 1024)


def _dot(a, b):
    return jnp.dot(a, b, preferred_element_type=F32)


def _dot_nt(a, b):
    return lax.dot_general(a, b, NT_DIMS, preferred_element_type=F32)


def _dot_tn(a, b):
    return lax.dot_general(a, b, TN_DIMS, preferred_element_type=F32)


def _rms(x, w):
    return x * lax.rsqrt(jnp.mean(x * x, axis=-1, keepdims=True) + EPS) * w


def _pick(n, target, mult=16):
    best = None
    for t in range(mult, min(n, target) + 1, mult):
        if n % t == 0:
            best = t
    assert best is not None, (n, target)
    return best


def _proj_kernel(x_ref, nw_ref, w_ref, wt_ref, of_ref, ob_ref, ot_ref, h_scr, *, NF):
    j = pl.program_id(1)

    @pl.when(j == 0)
    def _():
        h = _rms(x_ref[...], nw_ref[...]).astype(BF16)
        h_scr[...] = h
        ot_ref[...] = _dot(h, wt_ref[...])

    r = _dot(h_scr[...], w_ref[...])

    @pl.when(j < NF)
    def _():
        of_ref[...] = r

    @pl.when(j >= NF)
    def _():
        ob_ref[...] = r.astype(BF16)


def _proj(x, nw, w_main, w_tail):
    M = x.shape[0]
    tm = _pick(M, 1024)
    tn = 512
    NF = F32_COLS // tn
    return pl.pallas_call(
        functools.partial(_proj_kernel, NF=NF),
        grid=(M // tm, (F32_COLS + B16_COLS) // tn),
        in_specs=[pl.BlockSpec((tm, D), lambda i, j: (i, 0)),
                  pl.BlockSpec((1, D), lambda i, j: (0, 0)),
                  pl.BlockSpec((D, tn), lambda i, j: (0, j)),
                  pl.BlockSpec((D, TAIL_COLS), lambda i, j: (0, 0))],
        out_specs=[pl.BlockSpec((tm, tn), lambda i, j: (i, jnp.minimum(j, NF - 1))),
                   pl.BlockSpec((tm, tn), lambda i, j: (i, jnp.maximum(j - NF, 0))),
                   pl.BlockSpec((tm, TAIL_COLS), lambda i, j: (i, 0))],
        out_shape=[jax.ShapeDtypeStruct((M, F32_COLS), F32),
                   jax.ShapeDtypeStruct((M, B16_COLS), BF16),
                   jax.ShapeDtypeStruct((M, TAIL_COLS), F32)],
        scratch_shapes=[pltpu.VMEM((tm, D), BF16)],
        compiler_params=_cparams(("parallel", "arbitrary")),
        name="proj",
    )(x, nw, w_main, w_tail)


def _pool_kernel(u_ref, pre_ref, w_ref, sc_ref, y_ref, st_ref, ext_scr, pooled_scr, *, SB, T, NT, start_pos):
    t = pl.program_id(1)
    pos = start_pos + t * T + lax.broadcasted_iota(jnp.int32, (T, 1), 0)
    for s in range(SB):
        @pl.when(t == 0)
        def _():
            ext_scr[s, 1:16, :] = pre_ref[s]

        ext_scr[s, 16:16 + T, :] = u_ref[s * T:(s + 1) * T, :]
        for g, w in enumerate(POOL_WINDOWS):
            cols = slice(g * POOL_GW, (g + 1) * POOL_GW)
            u = ext_scr[s, 16:16 + T, cols]
            acc = u
            for j in range(1, w):
                acc = acc + ext_scr[s, 16 - j:16 - j + T, cols]
            cnt = jnp.minimum(pos + 1, w).astype(F32)
            pooled_scr[s * T:(s + 1) * T, cols] = (acc / cnt - u).astype(BF16)
        last = ext_scr[s, T + 1:T + 16, :]

        @pl.when(t == NT - 1)
        def _():
            st_ref[s] = last

        ext_scr[s, 1:16, :] = last
    for g in range(len(POOL_WINDOWS)):
        cols = slice(g * POOL_GW, (g + 1) * POOL_GW)
        y_ref[:, cols] = _dot(pooled_scr[:, cols], w_ref[g]) * sc_ref[:, cols]


def _pool(p1, prefix, w_pool, scale, *, B, L, start_pos):
    if L >= 512:
        SB, T = 1, 512
    else:
        SB, T = _pick(B, 16, 1), L
    NT = L // T
    kern = functools.partial(_pool_kernel, SB=SB, T=T, NT=NT, start_pos=start_pos)
    return pl.pallas_call(
        kern,
        grid=(B // SB, NT),
        in_specs=[pl.BlockSpec((SB * T, D), lambda g, t: (g * NT + t, 0)),
                  pl.BlockSpec((SB, POOL_STATE, D), lambda g, t: (g, 0, 0)),
                  pl.BlockSpec((4, POOL_GW, POOL_GW), lambda g, t: (0, 0, 0)),
                  pl.BlockSpec((1, D), lambda g, t: (0, 0))],
        out_specs=[pl.BlockSpec((SB * T, D), lambda g, t: (g * NT + t, 0)),
                   pl.BlockSpec((SB, POOL_STATE, D), lambda g, t: (g, 0, 0))],
        out_shape=[jax.ShapeDtypeStruct((B * L, D), F32),
                   jax.ShapeDtypeStruct((B, POOL_STATE, D), F32)],
        scratch_shapes=[pltpu.VMEM((SB, T + 16, D), F32), pltpu.VMEM((SB * T, D), BF16)],
        compiler_params=_cparams(("parallel", "arbitrary")),
        name="pool",
    )(p1, prefix, w_pool, scale)


def _gla_kernel(q_ref, k_ref, v_ref, r_ref, tail_ref, s0_ref, wa_ref, ba_ref, nw_ref,
                o_ref, sout_ref, st_scr, la_scr, *, SB, T, C, NT):
    t = pl.program_id(1)
    z = _dot(tail_ref[...].astype(BF16), wa_ref[...]) + ba_ref[...]
    la_scr[...] = (jnp.minimum(z, 0.0) - jnp.log(1.0 + jnp.exp(-jnp.abs(z)))) * (1.0 / GLA_TAU)

    @pl.when(t == 0)
    def _():
        for s in range(SB):
            for h in range(GLA_H):
                st_scr[s, h] = s0_ref[s, h].T

    row = lax.broadcasted_iota(jnp.int32, (C, C), 0)
    col = lax.broadcasted_iota(jnp.int32, (C, C), 1)
    tril = row >= col
    tri = tril.astype(F32).astype(BF16)
    nw = nw_ref[...]
    NC = T // C
    pre = [x[...].astype(F32) for x in (q_ref, k_ref, v_ref, r_ref)] if NC == 1 else None

    def load(i, ref, rows, cols=slice(None)):
        return pre[i][rows, cols] if pre is not None else ref[rows, cols].astype(F32)

    def chunk(s, c):
        if isinstance(c, int):
            rows = slice(s * T + c * C, s * T + (c + 1) * C)
        else:
            rows = pl.ds(pl.multiple_of(s * T + c * C, C), C)
        la = la_scr[rows, :]
        la_hi = la.astype(BF16)
        la_lo = (la - la_hi.astype(F32)).astype(BF16)
        b = _dot(tri, la_hi) + _dot(tri, la_lo)
        bl = b[C - 1:C, :]
        k = load(1, k_ref, rows)
        qt_all = (load(0, q_ref, rows) * (GLA_DK ** -0.5) * jnp.exp(b)).astype(BF16)
        kt_all = (k * jnp.exp(-b)).astype(BF16)
        kd_all = (k * jnp.exp(bl - b)).astype(BF16)
        dec = jnp.exp(bl)
        for h in range(GLA_H):
            kc = slice(h * GLA_DK, (h + 1) * GLA_DK)
            vc = slice(h * GLA_DV, (h + 1) * GLA_DV)
            qt = qt_all[:, kc]
            vb = load(2, v_ref, rows, vc).astype(BF16)
            att = jnp.where(tril, _dot_nt(qt, kt_all[:, kc]), 0.0)
            st = st_scr[s, h]
            o = _dot(att.astype(BF16), vb) + _dot_nt(qt, st.astype(BF16))
            st_scr[s, h] = st * dec[:, kc] + _dot_tn(vb, kd_all[:, kc])
            r = load(3, r_ref, rows, vc)
            o_ref[rows, vc] = _rms(o, nw) * (r * (1.0 / (1.0 + jnp.exp(-r))))

    for s in range(SB):
        if NC == 1:
            chunk(s, 0)
        else:
            def body(c, carry, s=s):
                chunk(s, c)
                return carry
            lax.fori_loop(0, NC, body, 0, unroll=4)

    @pl.when(t == NT - 1)
    def _():
        for s in range(SB):
            for h in range(GLA_H):
                sout_ref[s, h] = st_scr[s, h].T


def _gla(p1, p2, s0, wa, ba, nw, *, B, L):
    C = math.gcd(L, GLA_CHUNK)
    if L >= 512:
        SB, T = 1, 512
    else:
        SB, T = _pick(B, 8, 1), L
    NT = L // T
    R = SB * T
    kern = functools.partial(_gla_kernel, SB=SB, T=T, C=C, NT=NT)
    return pl.pallas_call(
        kern,
        grid=(B // SB, NT),
        in_specs=[pl.BlockSpec((R, 512), lambda g, t: (g * NT + t, 0)),
                  pl.BlockSpec((R, 512), lambda g, t: (g * NT + t, 1)),
                  pl.BlockSpec((R, D), lambda g, t: (g * NT + t, 1)),
                  pl.BlockSpec((R, D), lambda g, t: (g * NT + t, 2)),
                  pl.BlockSpec((R, TAIL_COLS), lambda g, t: (g * NT + t, 0)),
                  pl.BlockSpec((SB, GLA_H, GLA_DK, GLA_DV), lambda g, t: (g, 0, 0, 0)),
                  pl.BlockSpec((TAIL_COLS, GLA_H * GLA_DK), lambda g, t: (0, 0)),
                  pl.BlockSpec((1, GLA_H * GLA_DK), lambda g, t: (0, 0)),
                  pl.BlockSpec((1, GLA_DV), lambda g, t: (0, 0))],
        out_specs=[pl.BlockSpec((R, D), lambda g, t: (g * NT + t, 0)),
                   pl.BlockSpec((SB, GLA_H, GLA_DK, GLA_DV), lambda g, t: (g, 0, 0, 0))],
        out_shape=[jax.ShapeDtypeStruct((B * L, D), F32),
                   jax.ShapeDtypeStruct((B, GLA_H, GLA_DK, GLA_DV), F32)],
        scratch_shapes=[pltpu.VMEM((SB, GLA_H, GLA_DV, GLA_DK), F32),
                        pltpu.VMEM((R, GLA_H * GLA_DK), F32)],
        compiler_params=_cparams(("parallel", "arbitrary")),
        name="gla",
    )(p1, p1, p1, p1, p2, s0, wa, ba, nw)


def _rope128(x, cos, sin, lo):
    rot = jnp.where(lo, pltpu.roll(x, 96, 1), pltpu.roll(x, 32, 1))
    return x * cos + rot * sin


def _q_kernel(mq_ref, anw_ref, w_ref, qnw_ref, cos_ref, sin_ref, q_ref):
    xn = _rms(mq_ref[...], anw_ref[...]).astype(BF16)
    q = _dot(xn, w_ref[...])
    lo = lax.broadcasted_iota(jnp.int32, (1, LANES), 1) < 32
    cos = cos_ref[...]
    sin = sin_ref[...]
    wn = qnw_ref[:, 0:128]
    wr = qnw_ref[:, 128:256]
    for h in range(MLA_H):
        qn = q[:, h * HEAD_PAD:h * HEAD_PAD + 128]
        qr = q[:, h * HEAD_PAD + 128:(h + 1) * HEAD_PAD]
        ssq = jnp.sum(qn * qn, axis=-1, keepdims=True) + jnp.sum(qr * qr, axis=-1, keepdims=True)
        rinv = lax.rsqrt(ssq / MLA_DQK + EPS)
        qn = qn * rinv * wn
        qr = _rope128(qr * rinv * wr, cos, sin, lo)
        q_ref[:, h * HEAD_PAD:h * HEAD_PAD + 128] = (qn * ATTN_SCALE).astype(BF16)
        q_ref[:, h * HEAD_PAD + 128:(h + 1) * HEAD_PAD] = (qr * ATTN_SCALE).astype(BF16)


def _qpath(p1, anw, w_uq, qnw, cos, sin):
    M = p1.shape[0]
    tr = cos.shape[0]
    tm = _pick(math.gcd(M, tr), 512)
    nper = tr // tm
    return pl.pallas_call(
        _q_kernel,
        grid=(M // tm,),
        in_specs=[pl.BlockSpec((tm, Q_LORA), lambda i: (i, 4)),
                  pl.BlockSpec((1, Q_LORA), lambda i: (0, 0)),
                  pl.BlockSpec((Q_LORA, MLA_H * HEAD_PAD), lambda i: (0, 0)),
                  pl.BlockSpec((1, HEAD_PAD), lambda i: (0, 0)),
                  pl.BlockSpec((tm, LANES), lambda i: (i % nper, 0)),
                  pl.BlockSpec((tm, LANES), lambda i: (i % nper, 0))],
        out_specs=pl.BlockSpec((tm, MLA_H * HEAD_PAD), lambda i: (i, 0)),
        out_shape=jax.ShapeDtypeStruct((M, MLA_H * HEAD_PAD), BF16),
        compiler_params=_cparams(("parallel",)),
        name="qpath",
    )(p1, anw, w_uq, qnw, cos, sin)


def _kv_kernel(mkv_ref, tail_ref, anw_ref, wuk_ref, wuv_ref, kw_ref, cos_ref, sin_ref, c_ref, *kv_refs):
    c = _rms(mkv_ref[...], anw_ref[...])
    c_ref[...] = c
    if not kv_refs:
        return
    k_ref, v_ref = kv_refs
    cb = c.astype(BF16)
    kn = _dot(cb, wuk_ref[...])
    v_ref[...] = _dot(cb, wuv_ref[...]).astype(BF16)
    lane = lax.broadcasted_iota(jnp.int32, (1, LANES), 1)
    pe = jnp.where(lane < MLA_ROPE, tail_ref[...], 0.0)
    ssq_pe = jnp.sum(pe * pe, axis=-1, keepdims=True)
    kr = _rope128(pe * kw_ref[:, 128:256], cos_ref[...], sin_ref[...], lane < 32)
    wn = kw_ref[:, 0:128]
    for h in range(MLA_H):
        knh = kn[:, h * 128:(h + 1) * 128]
        rinv = lax.rsqrt((jnp.sum(knh * knh, axis=-1, keepdims=True) + ssq_pe) / MLA_DQK + EPS)
        k_ref[:, h * HEAD_PAD:h * HEAD_PAD + 128] = (knh * rinv * wn).astype(BF16)
        k_ref[:, h * HEAD_PAD + 128:(h + 1) * HEAD_PAD] = (kr * rinv).astype(BF16)


def _kvprep(p1, p2, anw, wuk, wuv, kw, cos, sin, *, with_kv):
    M = p1.shape[0]
    tr = cos.shape[0]
    tm = _pick(math.gcd(M, tr), 512)
    nper = tr // tm
    out_specs = [pl.BlockSpec((tm, KV_LORA), lambda i: (i, 0))]
    out_shape = [jax.ShapeDtypeStruct((M, KV_LORA), F32)]
    if with_kv:
        out_specs += [pl.BlockSpec((tm, MLA_H * HEAD_PAD), lambda i: (i, 0)),
                      pl.BlockSpec((tm, MLA_H * MLA_DV), lambda i: (i, 0))]
        out_shape += [jax.ShapeDtypeStruct((M, MLA_H * HEAD_PAD), BF16),
                      jax.ShapeDtypeStruct((M, MLA_H * MLA_DV), BF16)]
    return pl.pallas_call(
        _kv_kernel,
        grid=(M // tm,),
        in_specs=[pl.BlockSpec((tm, KV_LORA), lambda i: (i, 5)),
                  pl.BlockSpec((tm, TAIL_COLS), lambda i: (i, 0)),
                  pl.BlockSpec((1, KV_LORA), lambda i: (0, 0)),
                  pl.BlockSpec((KV_LORA, MLA_H * MLA_NOPE), lambda i: (0, 0)),
                  pl.BlockSpec((KV_LORA, MLA_H * MLA_DV), lambda i: (0, 0)),
                  pl.BlockSpec((1, HEAD_PAD), lambda i: (0, 0)),
                  pl.BlockSpec((tm, LANES), lambda i: (i % nper, 0)),
                  pl.BlockSpec((tm, LANES), lambda i: (i % nper, 0))],
        out_specs=out_specs,
        out_shape=out_shape,
        compiler_params=_cparams(("parallel",)),
        name="kvprep",
    )(p1, p2, anw, wuk, wuv, kw, cos, sin)


def _attn_kernel(q_ref, k_ref, v_ref, o_ref, s_scr, m_scr, l_scr, acc_scr, *, TQ, TK):
    R = TQ // TK
    HL = TK // LANES
    nfull = pl.program_id(2) * R

    def fold(x, op):
        y = x[:, 0:LANES]
        for g in range(1, HL):
            y = op(y, x[:, g * LANES:(g + 1) * LANES])
        return y

    def kv_rows(j):
        return pl.ds(pl.multiple_of(j * TK, TK), TK)

    m_scr[...] = jnp.full((TQ, LANES), -jnp.inf, F32)

    def full_a(j, carry):
        s = _dot_nt(q_ref[...], k_ref[kv_rows(j), :])
        s_scr[j] = s
        m_scr[...] = jnp.maximum(m_scr[...], fold(s, jnp.maximum))
        return carry

    lax.fori_loop(0, nfull, full_a, 0)
    tri = lax.broadcasted_iota(jnp.int32, (TK, TK), 1) <= lax.broadcasted_iota(jnp.int32, (TK, TK), 0)
    for d in range(R):
        r0 = d * TK
        j = nfull + d
        s = _dot_nt(q_ref[r0:, :], k_ref[kv_rows(j), :])
        sd = jnp.where(tri, s[0:TK], -jnp.inf)
        s_scr[j, r0:r0 + TK, :] = sd
        m_scr[r0:r0 + TK, :] = jnp.maximum(m_scr[r0:r0 + TK, :], fold(sd, jnp.maximum))
        if r0 + TK < TQ:
            s_scr[j, r0 + TK:, :] = s[TK:]
            m_scr[r0 + TK:, :] = jnp.maximum(m_scr[r0 + TK:, :], fold(s[TK:], jnp.maximum))

    m_scr[...] = jnp.broadcast_to(jnp.max(m_scr[...], axis=-1, keepdims=True), (TQ, LANES))
    l_scr[...] = jnp.zeros((TQ, LANES), F32)
    acc_scr[...] = jnp.zeros((TQ, MLA_DV), F32)

    def probs(s, m):
        return jnp.concatenate([jnp.exp(s[:, g * LANES:(g + 1) * LANES] - m) for g in range(HL)], axis=1)

    def full_b(j, carry):
        p = probs(s_scr[j], m_scr[...])
        l_scr[...] += fold(p, jnp.add)
        acc_scr[...] += _dot(p.astype(BF16), v_ref[kv_rows(j), :])
        return carry

    lax.fori_loop(0, nfull, full_b, 0)
    for d in range(R):
        r0 = d * TK
        j = nfull + d
        p = probs(s_scr[j, r0:, :], m_scr[r0:, :])
        l_scr[r0:, :] += fold(p, jnp.add)
        acc_scr[r0:, :] += _dot(p.astype(BF16), v_ref[kv_rows(j), :])
    o_ref[...] = acc_scr[...] / jnp.sum(l_scr[...], axis=-1, keepdims=True)


def _attn(q, k, v, *, B, L):
    TK = 256
    TQ = _pick(L, 1024, TK)
    NQ = L // TQ
    kern = functools.partial(_attn_kernel, TQ=TQ, TK=TK)
    return pl.pallas_call(
        kern,
        grid=(B, MLA_H, NQ),
        in_specs=[pl.BlockSpec((TQ, HEAD_PAD), lambda b, h, i: (b * NQ + i, h)),
                  pl.BlockSpec((L, HEAD_PAD), lambda b, h, i: (b, h)),
                  pl.BlockSpec((L, MLA_DV), lambda b, h, i: (b, h))],
        out_specs=pl.BlockSpec((TQ, MLA_DV), lambda b, h, i: (b * NQ + i, h)),
        out_shape=jax.ShapeDtypeStruct((B * L, MLA_H * MLA_DV), F32),
        scratch_shapes=[pltpu.VMEM((L // TK, TQ, TK), F32), pltpu.VMEM((TQ, LANES), F32),
                        pltpu.VMEM((TQ, LANES), F32), pltpu.VMEM((TQ, MLA_DV), F32)],
        compiler_params=_cparams(("parallel", "parallel", "arbitrary")),
        name="attn",
    )(q, k, v)


def _qs_kernel(q_ref, wukT_ref, kw_ref, qlat_ref, a_ref, *, DB, DL):
    q = q_ref[...].astype(F32)
    qn = (q[:, 0:128] * kw_ref[:, 0:128]).astype(BF16)
    qlat_ref[...] = _dot(qn, wukT_ref[...]).reshape(DB, 1, DL, KV_LORA)
    qr = q[:, 128:256]
    wr = kw_ref[:, 128:256]
    lo = lax.broadcasted_iota(jnp.int32, (1, LANES), 1) < 32
    a1 = qr * wr
    a2 = jnp.where(lo, pltpu.roll(qr, 96, 1), -pltpu.roll(qr, 32, 1)) * wr
    a_ref[...] = (a1 + pltpu.roll(a2, 64, 1)).reshape(DB, 1, DL, LANES)


def _qs(q_s, wukT, kw, *, DB, DL):
    kern = functools.partial(_qs_kernel, DB=DB, DL=DL)
    return pl.pallas_call(
        kern,
        grid=(MLA_H,),
        in_specs=[pl.BlockSpec((DB * DL, HEAD_PAD), lambda h: (0, h)),
                  pl.BlockSpec((MLA_NOPE, KV_LORA), lambda h: (h, 0)),
                  pl.BlockSpec((1, HEAD_PAD), lambda h: (0, 0))],
        out_specs=[pl.BlockSpec((DB, 1, DL, KV_LORA), lambda h: (0, h, 0, 0)),
                   pl.BlockSpec((DB, 1, DL, LANES), lambda h: (0, h, 0, 0))],
        out_shape=[jax.ShapeDtypeStruct((DB, MLA_H, DL, KV_LORA), F32),
                   jax.ShapeDtypeStruct((DB, MLA_H, DL, LANES), F32)],
        compiler_params=_cparams(("parallel",)),
        name="qs",
    )(q_s, wukT, kw)


def _decode_kernel(pt_ref, *refs, PPS, NS, DL, CP):
    lat_refs = refs[:PPS]
    pe_refs = refs[PPS:2 * PPS]
    (cst_ref, csn_ref, qlat_ref, a_ref, wukT_ref, cnew_ref, tnew_ref, wuv_ref, y_ref,
     wall_scr, cb_scr, k2t_scr, m_scr, l_scr, acc_scr) = refs[2 * PPS:]
    del pt_ref
    b = pl.program_id(0)
    s = pl.program_id(1)
    NR = MLA_H * DL
    CK = CP * PAGE

    @pl.when((b == 0) & (s == 0))
    def _():
        wall_scr[0:MLA_H * MLA_NOPE, :] = wukT_ref[...]

    @pl.when(s == 0)
    def _():
        wall_scr[MLA_H * MLA_NOPE:, :] = qlat_ref[...].reshape(NR, KV_LORA).astype(BF16)
        m_scr[...] = jnp.full((NR, 1), -jnp.inf, F32)
        l_scr[...] = jnp.zeros((NR, 1), F32)
        acc_scr[...] = jnp.zeros((NR, KV_LORA), F32)

    a = a_ref[...].reshape(NR, LANES).astype(BF16)

    def scores(cb, s_rope, ssq_pe):
        n = cb.shape[0]
        big = _dot_nt(wall_scr[...], cb)
        kn = big[0:MLA_H * MLA_NOPE, :].reshape(MLA_H, MLA_NOPE, n)
        rinv = lax.rsqrt((jnp.sum(kn * kn, axis=1) + ssq_pe) / MLA_DQK + EPS)
        sc = big[MLA_H * MLA_NOPE:, :] + s_rope
        return (sc.reshape(MLA_H, DL, n) * rinv[:, None, :]).reshape(NR, n)

    def update(sc, vals):
        m_old = m_scr[...]
        m_new = jnp.maximum(m_old, jnp.max(sc, axis=-1, keepdims=True))
        alpha = jnp.exp(m_old - m_new)
        p = jnp.exp(sc - m_new)
        l_scr[...] = l_scr[...] * alpha + jnp.sum(p, axis=-1, keepdims=True)
        acc_scr[...] = acc_scr[...] * alpha + _dot(p.astype(BF16), vals)
        m_scr[...] = m_new

    @pl.when(s < NS)
    def _():
        chunks = []
        for c in range(PPS // CP):
            sq = []
            for i in range(c * CP, (c + 1) * CP):
                keys = slice(i * PAGE, (i + 1) * PAGE)
                cb_scr[keys, :] = lat_refs[i][...].astype(BF16)
                pet = pe_refs[i][...]
                k2t_scr[0:MLA_ROPE, keys] = (pet * cst_ref[0:MLA_ROPE, keys]).astype(BF16)
                k2t_scr[MLA_ROPE:, keys] = (pet * cst_ref[MLA_ROPE:, keys]).astype(BF16)
                sq.append(jnp.sum(pet * pet, axis=0, keepdims=True))
            ck = slice(c * CK, (c + 1) * CK)
            s_rope = _dot(a, k2t_scr[:, ck])
            chunks.append((scores(cb_scr[ck, :], s_rope, jnp.concatenate(sq, axis=1)), ck))
        for sc, ck in chunks:
            update(sc, cb_scr[ck, :])

    @pl.when(s == NS)
    def _():
        lane = lax.broadcasted_iota(jnp.int32, (1, LANES), 1)
        pad = lambda x: jnp.concatenate([x, jnp.zeros((PAGE - DL, x.shape[1]), F32)], axis=0)
        cn = pad(cnew_ref[...]).astype(BF16)
        pe = jnp.where(lane < MLA_ROPE, tnew_ref[...], 0.0)
        k2 = pad((pe + pltpu.roll(pe, 64, 1)) * csn_ref[...]).astype(BF16)
        sq = pad(pe * pe)
        sqh = sq.astype(BF16)
        sql = (sq - sqh.astype(F32)).astype(BF16)
        ones = jnp.ones((MLA_H, LANES), BF16)
        sc = scores(cn, _dot_nt(a, k2), _dot_nt(ones, sqh) + _dot_nt(ones, sql))
        qidx = lax.broadcasted_iota(jnp.int32, (MLA_H, DL, PAGE), 1).reshape(NR, PAGE)
        kidx = lax.broadcasted_iota(jnp.int32, (NR, PAGE), 1)
        update(jnp.where(kidx <= qidx, sc, -jnp.inf), cn)
        o = (acc_scr[...] / l_scr[...]).astype(BF16)
        full = _dot(o, wuv_ref[...])
        for h in range(MLA_H):
            y_ref[:, h * MLA_DV:(h + 1) * MLA_DV] = full[h * DL:(h + 1) * DL, h * MLA_DV:(h + 1) * MLA_DV]


def _decode(page_table, cache_lat, cache_pet, layer, cst, csn, qlat, a, wukT, c_new, tail_new, wuv, *, DB, DL, PPS):
    n_pages = page_table.shape[1]
    NS = n_pages // PPS
    TK = PPS * PAGE
    NR = MLA_H * DL
    CP = math.gcd(PPS, 4)

    def page_map(i):
        return lambda b, s, pt: (layer, pt[b, jnp.minimum(s, NS - 1) * PPS + i], 0, 0)

    in_specs = ([pl.BlockSpec((None, None, PAGE, KV_LORA), page_map(i)) for i in range(PPS)]
                + [pl.BlockSpec((None, None, MLA_ROPE, PAGE), page_map(i)) for i in range(PPS)]
                + [pl.BlockSpec((LANES, TK), lambda b, s, pt: (0, jnp.minimum(s, NS - 1))),
                   pl.BlockSpec((DL, LANES), lambda b, s, pt: (0, 0)),
                   pl.BlockSpec((None, MLA_H, DL, KV_LORA), lambda b, s, pt: (b, 0, 0, 0)),
                   pl.BlockSpec((None, MLA_H, DL, LANES), lambda b, s, pt: (b, 0, 0, 0)),
                   pl.BlockSpec((MLA_H * MLA_NOPE, KV_LORA), lambda b, s, pt: (0, 0)),
                   pl.BlockSpec((DL, KV_LORA), lambda b, s, pt: (b, 0)),
                   pl.BlockSpec((DL, TAIL_COLS), lambda b, s, pt: (b, 0)),
                   pl.BlockSpec((KV_LORA, MLA_H * MLA_DV), lambda b, s, pt: (0, 0))])
    grid_spec = pltpu.PrefetchScalarGridSpec(
        num_scalar_prefetch=1,
        grid=(DB, NS + 1),
        in_specs=in_specs,
        out_specs=pl.BlockSpec((DL, MLA_H * MLA_DV), lambda b, s, pt: (b, 0)),
        scratch_shapes=[pltpu.VMEM((MLA_H * MLA_NOPE + NR, KV_LORA), BF16),
                        pltpu.VMEM((TK, KV_LORA), BF16),
                        pltpu.VMEM((LANES, TK), BF16),
                        pltpu.VMEM((NR, 1), F32),
                        pltpu.VMEM((NR, 1), F32),
                        pltpu.VMEM((NR, KV_LORA), F32)])
    kern = functools.partial(_decode_kernel, PPS=PPS, NS=NS, DL=DL, CP=CP)
    return pl.pallas_call(
        kern,
        grid_spec=grid_spec,
        out_shape=jax.ShapeDtypeStruct((DB * DL, MLA_H * MLA_DV), F32),
        compiler_params=_cparams(("arbitrary", "arbitrary")),
        name="decode",
    )(page_table, *([cache_lat] * PPS), *([cache_pet] * PPS), cst, csn, qlat, a, wukT, c_new, tail_new, wuv)


def _merge_kernel(g0_ref, g1_ref, g2_ref, bg_ref, yp_ref, yg_ref, ym_ref, x_ref, w_ref, o_ref):
    def gate(g_ref, i):
        return 1.0 / (1.0 + jnp.exp(-(g_ref[...].astype(F32) + bg_ref[i:i + 1, :])))

    merged = gate(g0_ref, 0) * yp_ref[...] + gate(g1_ref, 1) * yg_ref[...] + gate(g2_ref, 2) * ym_ref[...]
    o_ref[...] = x_ref[...] + _dot(merged.astype(BF16), w_ref[...])


def _merge(p1, bg, yp, yg, ym, x, w_out):
    M = x.shape[0]
    tm = _pick(M, 512)
    row = lambda i: (i, 0)
    return pl.pallas_call(
        _merge_kernel,
        grid=(M // tm,),
        in_specs=[pl.BlockSpec((tm, D), lambda i: (i, 3)),
                  pl.BlockSpec((tm, D), lambda i: (i, 4)),
                  pl.BlockSpec((tm, D), lambda i: (i, 5)),
                  pl.BlockSpec((3, D), lambda i: (0, 0)),
                  pl.BlockSpec((tm, D), row), pl.BlockSpec((tm, D), row), pl.BlockSpec((tm, D), row),
                  pl.BlockSpec((tm, D), row),
                  pl.BlockSpec((D, D), lambda i: (0, 0))],
        out_specs=pl.BlockSpec((tm, D), row),
        out_shape=jax.ShapeDtypeStruct((M, D), F32),
        compiler_params=_cparams(("parallel",)),
        name="merge",
    )(p1, p1, p1, bg, yp, yg, ym, x, w_out)


def _mlp_kernel(x_ref, nw_ref, wu_ref, wd_ref, o_ref, h_scr, acc_scr):
    f = pl.program_id(1)

    @pl.when(f == 0)
    def _():
        h_scr[...] = _rms(x_ref[...], nw_ref[...]).astype(BF16)
        acc_scr[...] = jnp.zeros_like(acc_scr)

    a = jnp.maximum(_dot(h_scr[...], wu_ref[...]), 0.0)
    acc_scr[...] += _dot((a * a).astype(BF16), wd_ref[...])

    @pl.when(f == pl.num_programs(1) - 1)
    def _():
        o_ref[...] = x_ref[...] + acc_scr[...]


def _mlp(x, nw, w_up, w_down):
    M = x.shape[0]
    tm = _pick(M, 1024)
    tf = 512
    return pl.pallas_call(
        _mlp_kernel,
        grid=(M // tm, D_FF // tf),
        in_specs=[pl.BlockSpec((tm, D), lambda i, f: (i, 0)),
                  pl.BlockSpec((1, D), lambda i, f: (0, 0)),
                  pl.BlockSpec((D, tf), lambda i, f: (0, f)),
                  pl.BlockSpec((tf, D), lambda i, f: (f, 0))],
        out_specs=pl.BlockSpec((tm, D), lambda i, f: (i, 0)),
        out_shape=jax.ShapeDtypeStruct((M, D), F32),
        scratch_shapes=[pltpu.VMEM((tm, D), BF16), pltpu.VMEM((tm, D), F32)],
        compiler_params=_cparams(("parallel", "arbitrary")),
        name="mlp",
    )(x, nw, w_up, w_down)


def _rope_angles(pos):
    inv = 1.0 / (ROPE_THETA ** (jnp.arange(0, MLA_ROPE, 2, dtype=F32) / MLA_ROPE))
    return pos.astype(F32)[:, None] * inv[None, :]


def _token_tables(pos):
    ang = _rope_angles(pos)
    c, s = jnp.cos(ang), jnp.sin(ang)
    z = jnp.zeros((pos.shape[0], LANES - MLA_ROPE), F32)
    return jnp.concatenate([c, c, z], axis=1), jnp.concatenate([-s, s, z], axis=1)


def _key_table(pos):
    ang = _rope_angles(pos)
    c, s = jnp.cos(ang), jnp.sin(ang)
    return jnp.concatenate([c, c, s, s], axis=1)


def _pad_heads(w):
    lead = w.shape[:-1]
    w = w.reshape(lead + (MLA_H, MLA_DQK))
    w = jnp.concatenate([w, jnp.zeros(lead + (MLA_H, HEAD_PAD - MLA_DQK), w.dtype)], axis=-1)
    return w.reshape(lead + (MLA_H * HEAD_PAD,))


def kernel(x_prompt, x_sample, state_pool, state_gla, cache_kv_latent, cache_k_rope, page_table,
           norm1_w, w_in, b_gate, w_pool, pool_scale, w_gla_a2, b_gla_a, gla_norm_w,
           q_a_norm_w, kv_a_norm_w, w_uq, q_norm_w, w_uk, w_uv, k_norm_w, w_out,
           norm2_w, w_up, w_down):
    return _forward(x_prompt, x_sample, state_pool, state_gla, cache_kv_latent, cache_k_rope, page_table,
                    norm1_w, w_in, b_gate, w_pool, pool_scale, w_gla_a2, b_gla_a, gla_norm_w,
                    q_a_norm_w, kv_a_norm_w, w_uq, q_norm_w, w_uk, w_uv, k_norm_w, w_out,
                    norm2_w, w_up, w_down, pages_per_step=16)


def _forward(x_prompt, x_sample, state_pool, state_gla, cache_kv_latent, cache_k_rope, page_table,
             norm1_w, w_in, b_gate, w_pool, pool_scale, w_gla_a2, b_gla_a, gla_norm_w,
             q_a_norm_w, kv_a_norm_w, w_uq, q_norm_w, w_uk, w_uv, k_norm_w, w_out,
             norm2_w, w_up, w_down, *, pages_per_step):
    B, L, _ = x_prompt.shape
    DB, DL, _ = x_sample.shape
    depth = w_in.shape[0]
    n_pages = page_table.shape[1]
    past = n_pages * PAGE
    PPS = pages_per_step
    assert n_pages % PPS == 0

    o = [0]
    for n in (1024, 512, 512, 1024, GLA_RANK, 1024, Q_LORA, KV_LORA, MLA_ROPE, 3 * D):
        o.append(o[-1] + n)
    seg = lambda i: w_in[:, :, o[i]:o[i + 1]]
    w_main = jnp.concatenate([seg(0), seg(6), seg(7), seg(1), seg(2), seg(3), seg(5), seg(9)], axis=-1).astype(BF16)
    w_tail = jnp.concatenate([seg(8), seg(4), jnp.zeros((depth, D, TAIL_COLS - MLA_ROPE - GLA_RANK), F32)],
                             axis=-1).astype(BF16)
    wa_pad = jnp.zeros((depth, TAIL_COLS, GLA_H * GLA_DK), F32).at[:, MLA_ROPE:MLA_ROPE + GLA_RANK, :].set(w_gla_a2)
    wa_pad = wa_pad.astype(BF16)
    w_pool_b = w_pool.astype(BF16)
    w_uq_p = _pad_heads(w_uq).astype(BF16)
    qnw_p = jnp.concatenate([q_norm_w, jnp.zeros((depth, HEAD_PAD - MLA_DQK), F32)], axis=-1)[:, None, :]
    knw_p = jnp.concatenate([k_norm_w, jnp.zeros((depth, HEAD_PAD - MLA_DQK), F32)], axis=-1)[:, None, :]
    w_uk_b = w_uk.astype(BF16)
    w_ukT_b = jnp.swapaxes(w_uk, 1, 2).astype(BF16)
    w_uv_b = w_uv.astype(BF16)
    w_out_b = w_out.astype(BF16)
    w_up_b = w_up.astype(BF16)
    w_down_b = w_down.astype(BF16)

    cos_p, sin_p = _token_tables(jnp.arange(L))
    reps = math.gcd(DB, 512 // DL)
    cos_s, sin_s = _token_tables(jnp.tile(past + jnp.arange(DL), reps))
    cs_past_t = _key_table(jnp.arange(past)).T
    cs_new = _key_table(past + jnp.arange(DL))
    cache_pet = jnp.swapaxes(cache_k_rope, 2, 3)

    xp = x_prompt.reshape(B * L, D)
    xs = x_sample.reshape(DB * DL, D)
    zero_prefix = jnp.zeros((B, POOL_STATE, D), F32)
    zero_state = jnp.zeros((B, GLA_H, GLA_DK, GLA_DV), F32)

    outs = [[] for _ in range(8)]
    for l in range(depth):
        r1 = lambda w: w[l][None, :]
        p1, pb, p2 = _proj(xp, r1(norm1_w), w_main[l], w_tail[l])
        yp, pool_p = _pool(p1, zero_prefix, w_pool_b[l], r1(pool_scale), B=B, L=L, start_pos=0)
        yg, gla_p = _gla(pb, p2, zero_state, wa_pad[l], r1(b_gla_a), r1(gla_norm_w), B=B, L=L)
        q = _qpath(p1, r1(q_a_norm_w), w_uq_p[l], qnw_p[l], cos_p, sin_p)
        c_p, k, v = _kvprep(p1, p2, r1(kv_a_norm_w), w_uk_b[l], w_uv_b[l], knw_p[l], cos_p, sin_p, with_kv=True)
        ym = _attn(q, k, v, B=B, L=L)
        xp = _merge(pb, b_gate[l], yp, yg, ym, xp, w_out_b[l])
        xp = _mlp(xp, r1(norm2_w), w_up_b[l], w_down_b[l])
        pe_p = p2[:, :MLA_ROPE]
        s1, sb, s2 = _proj(xs, r1(norm1_w), w_main[l], w_tail[l])
        yp, pool_s = _pool(s1, state_pool[l], w_pool_b[l], r1(pool_scale), B=DB, L=DL, start_pos=past)
        yg, gla_s = _gla(sb, s2, state_gla[l], wa_pad[l], r1(b_gla_a), r1(gla_norm_w), B=DB, L=DL)
        q = _qpath(s1, r1(q_a_norm_w), w_uq_p[l], qnw_p[l], cos_s, sin_s)
        (c_s,) = _kvprep(s1, s2, r1(kv_a_norm_w), w_uk_b[l], w_uv_b[l], knw_p[l], cos_s, sin_s, with_kv=False)
        qlat, a = _qs(q, w_ukT_b[l], knw_p[l], DB=DB, DL=DL)
        ym = _decode(page_table, cache_kv_latent, cache_pet, l, cs_past_t, cs_new, qlat, a, w_ukT_b[l], c_s, s2,
                     w_uv_b[l], DB=DB, DL=DL, PPS=PPS)
        xs = _merge(sb, b_gate[l], yp, yg, ym, xs, w_out_b[l])
        xs = _mlp(xs, r1(norm2_w), w_up_b[l], w_down_b[l])
        pe_s = s2[:, :MLA_ROPE]
        for lst, val in zip(outs, (pool_p, pool_s, gla_p, gla_s,
                                   c_p.reshape(B, L, KV_LORA), c_s.reshape(DB, DL, KV_LORA),
                                   pe_p.reshape(B, L, MLA_ROPE), pe_s.reshape(DB, DL, MLA_ROPE))):
            lst.append(val)
    return (xp.reshape(B, L, D), xs.reshape(DB, DL, D)) + tuple(jnp.stack(v) for v in outs)
```

```python
import functools
import math

import jax
import jax.numpy as jnp
from jax import lax
from jax.experimental import pallas as pl
from jax.experimental.pallas import tpu as pltpu

F32 = jnp.float32
BF16 = jnp.bfloat16
EPS = 1e-6

D = 1024
POOL_WINDOWS = (2, 4, 8, 16)
POOL_GW = 256
POOL_STATE = 15
POOL_HDR = 32
GLA_H = 4
GLA_DK = 128
GLA_DV = 256
GLA_RANK = 16
GLA_TAU = 16.0
GLA_CHUNK = 64
MLA_H = 8
MLA_NOPE = 128
MLA_ROPE = 64
MLA_DQK = MLA_NOPE + MLA_ROPE
MLA_DV = 128
Q_LORA = 256
KV_LORA = 256
ROPE_THETA = 10000.0
ATTN_SCALE = MLA_DQK ** -0.5
Q_SCALE = ATTN_SCALE * math.log2(math.e)
D_FF = 4 * D
PAGE = 128
HEAD_PAD = 256
LANES = 128

F32_COLS = 1536
B16_COLS = 6144
TAIL_COLS = 128

NT_DIMS = (((1,), (1,)), ((), ()))
TN_DIMS = (((0,), (0,)), ((), ()))


def _cparams(sem, vmem_mb=48):
    return pltpu.CompilerParams(dimension_semantics=sem, vmem_limit_bytes=vmem_mb * 1024 * 1024)


def _dot(a, b):
    return jnp.dot(a, b, preferred_element_type=F32)


def _dot_nt(a, b):
    return lax.dot_general(a, b, NT_DIMS, preferred_element_type=F32)


def _dot_tn(a, b):
    return lax.dot_general(a, b, TN_DIMS, preferred_element_type=F32)


def _rms(x, w):
    return x * lax.rsqrt(jnp.mean(x * x, axis=-1, keepdims=True) + EPS) * w


def _pick(n, target, mult=16):
    best = None
    for t in range(mult, min(n, target) + 1, mult):
        if n % t == 0:
            best = t
    assert best is not None, (n, target)
    return best


def _proj_kernel(x_ref, nw_ref, w_ref, wt_ref, of_ref, ob_ref, ot_ref, h_scr, *, NF):
    j = pl.program_id(1)

    @pl.when(j == 0)
    def _():
        h = _rms(x_ref[...], nw_ref[...]).astype(BF16)
        h_scr[...] = h
        ot_ref[...] = _dot(h, wt_ref[...])

    r = _dot(h_scr[...], w_ref[...])

    @pl.when(j < NF)
    def _():
        of_ref[...] = r

    @pl.when(j >= NF)
    def _():
        ob_ref[...] = r.astype(BF16)


def _proj(x, nw, w_main, w_tail):
    M = x.shape[0]
    tm = _pick(M, 1024)
    tn = 512
    NF = F32_COLS // tn
    return pl.pallas_call(
        functools.partial(_proj_kernel, NF=NF),
        grid=(M // tm, (F32_COLS + B16_COLS) // tn),
        in_specs=[pl.BlockSpec((tm, D), lambda i, j: (i, 0)),
                  pl.BlockSpec((1, D), lambda i, j: (0, 0)),
                  pl.BlockSpec((D, tn), lambda i, j: (0, j)),
                  pl.BlockSpec((D, TAIL_COLS), lambda i, j: (0, 0))],
        out_specs=[pl.BlockSpec((tm, tn), lambda i, j: (i, jnp.minimum(j, NF - 1))),
                   pl.BlockSpec((tm, tn), lambda i, j: (i, jnp.maximum(j - NF, 0))),
                   pl.BlockSpec((tm, TAIL_COLS), lambda i, j: (i, 0))],
        out_shape=[jax.ShapeDtypeStruct((M, F32_COLS), F32),
                   jax.ShapeDtypeStruct((M, B16_COLS), BF16),
                   jax.ShapeDtypeStruct((M, TAIL_COLS), F32)],
        scratch_shapes=[pltpu.VMEM((tm, D), BF16)],
        compiler_params=_cparams(("parallel", "arbitrary")),
        name="proj",
    )(x, nw, w_main, w_tail)


def _pool_kernel(u_ref, pre_ref, w_ref, sc_ref, y_ref, st_ref, ext_scr, lvl_scr, pooled_scr,
                 *, SB, T, NT, start_pos):
    t = pl.program_id(1)
    H, R0, GW = POOL_HDR, 16, POOL_GW
    pos = start_pos + t * T + lax.broadcasted_iota(jnp.int32, (T, 1), 0)

    @pl.when(t == 0)
    def _():
        lvl_scr[:, 0:R0, :] = jnp.zeros((3, R0, D), F32)

    for s in range(SB):
        @pl.when(t == 0)
        def _():
            ext_scr[s, 0:H - POOL_STATE, :] = jnp.zeros((H - POOL_STATE, D), F32)
            ext_scr[s, H - POOL_STATE:H, :] = pre_ref[s]

        ext_scr[s, H:H + T, :] = u_ref[s * T:(s + 1) * T, :]
        a1 = ext_scr[s, R0:H + T, :] + ext_scr[s, R0 - 1:H + T - 1, :]
        lvl_scr[0, R0:H + T, :] = a1
        a2 = a1[:, GW:] + lvl_scr[0, R0 - 2:H + T - 2, GW:]
        lvl_scr[1, R0:H + T, GW:] = a2
        a3 = a2[:, GW:] + lvl_scr[1, R0 - 4:H + T - 4, 2 * GW:]
        lvl_scr[2, R0:H + T, 2 * GW:] = a3
        a4 = a3[H - R0:, GW:] + lvl_scr[2, H - 8:H + T - 8, 3 * GW:]
        wsum = (a1[H - R0:, 0:GW], a2[H - R0:, 0:GW], a3[H - R0:, 0:GW], a4)
        for g, w in enumerate(POOL_WINDOWS):
            cols = slice(g * GW, (g + 1) * GW)
            cnt = jnp.minimum(pos + 1, w).astype(F32)
            pooled_scr[s * T:(s + 1) * T, cols] = (wsum[g] / cnt - ext_scr[s, H:H + T, cols]).astype(BF16)
        last = ext_scr[s, H + T - POOL_STATE:H + T, :]

        @pl.when(t == NT - 1)
        def _():
            st_ref[s] = last

        ext_scr[s, H - POOL_STATE:H, :] = last
    for g in range(len(POOL_WINDOWS)):
        cols = slice(g * POOL_GW, (g + 1) * POOL_GW)
        y_ref[:, cols] = (_dot(pooled_scr[:, cols], w_ref[g]) * sc_ref[:, cols]).astype(y_ref.dtype)


def _pool(p1, prefix, w_pool, scale, *, B, L, start_pos):
    if L >= 512:
        SB, T = 1, 512
    else:
        SB, T = _pick(B, 16, 1), L
    NT = L // T
    kern = functools.partial(_pool_kernel, SB=SB, T=T, NT=NT, start_pos=start_pos)
    return pl.pallas_call(
        kern,
        grid=(B // SB, NT),
        in_specs=[pl.BlockSpec((SB * T, D), lambda g, t: (g * NT + t, 0)),
                  pl.BlockSpec((SB, POOL_STATE, D), lambda g, t: (g, 0, 0)),
                  pl.BlockSpec((4, POOL_GW, POOL_GW), lambda g, t: (0, 0, 0)),
                  pl.BlockSpec((1, D), lambda g, t: (0, 0))],
        out_specs=[pl.BlockSpec((SB * T, D), lambda g, t: (g * NT + t, 0)),
                   pl.BlockSpec((SB, POOL_STATE, D), lambda g, t: (g, 0, 0))],
        out_shape=[jax.ShapeDtypeStruct((B * L, D), BF16),
                   jax.ShapeDtypeStruct((B, POOL_STATE, D), F32)],
        scratch_shapes=[pltpu.VMEM((SB, POOL_HDR + T, D), F32), pltpu.VMEM((3, POOL_HDR + T, D), F32),
                        pltpu.VMEM((SB * T, D), BF16)],
        compiler_params=_cparams(("parallel", "arbitrary")),
        name="pool",
    )(p1, prefix, w_pool, scale)


def _gla_kernel(q_ref, k_ref, v_ref, r_ref, tail_ref, s0_ref, wa_ref, ba_ref, nw_ref,
                o_ref, sout_ref, st_scr, la_scr, *, SB, T, C, NT):
    t = pl.program_id(1)
    z = _dot(tail_ref[...].astype(BF16), wa_ref[...]) + ba_ref[...]
    la_scr[...] = (jnp.minimum(z, 0.0) - jnp.log(1.0 + jnp.exp(-jnp.abs(z)))) * (1.0 / GLA_TAU)

    @pl.when(t == 0)
    def _():
        for s in range(SB):
            for h in range(GLA_H):
                st_scr[s, h] = s0_ref[s, h].T

    row = lax.broadcasted_iota(jnp.int32, (C, C), 0)
    col = lax.broadcasted_iota(jnp.int32, (C, C), 1)
    tril = row >= col
    tri = tril.astype(F32).astype(BF16)
    nw = nw_ref[...]
    NC = T // C
    pre = [x[...].astype(F32) for x in (q_ref, k_ref, v_ref, r_ref)] if NC == 1 else None

    def load(i, ref, rows, cols=slice(None)):
        return pre[i][rows, cols] if pre is not None else ref[rows, cols].astype(F32)

    def chunk(s, c):
        if isinstance(c, int):
            rows = slice(s * T + c * C, s * T + (c + 1) * C)
        else:
            rows = pl.ds(pl.multiple_of(s * T + c * C, C), C)
        la = la_scr[rows, :]
        la_hi = la.astype(BF16)
        la_lo = (la - la_hi.astype(F32)).astype(BF16)
        b = _dot(tri, la_hi) + _dot(tri, la_lo)
        bl = b[C - 1:C, :]
        k = load(1, k_ref, rows)
        qt_all = (load(0, q_ref, rows) * (GLA_DK ** -0.5) * jnp.exp(b)).astype(BF16)
        kt_all = (k * jnp.exp(-b)).astype(BF16)
        kd_all = (k * jnp.exp(bl - b)).astype(BF16)
        dec = jnp.exp(bl)
        for h in range(GLA_H):
            kc = slice(h * GLA_DK, (h + 1) * GLA_DK)
            vc = slice(h * GLA_DV, (h + 1) * GLA_DV)
            qt = qt_all[:, kc]
            vb = load(2, v_ref, rows, vc).astype(BF16)
            att = jnp.where(tril, _dot_nt(qt, kt_all[:, kc]), 0.0)
            st = st_scr[s, h]
            o = _dot(att.astype(BF16), vb) + _dot_nt(qt, st.astype(BF16))
            st_scr[s, h] = st * dec[:, kc] + _dot_tn(vb, kd_all[:, kc])
            r = load(3, r_ref, rows, vc)
            o_ref[rows, vc] = (_rms(o, nw) * (r * (1.0 / (1.0 + jnp.exp(-r))))).astype(o_ref.dtype)

    for s in range(SB):
        if NC == 1:
            chunk(s, 0)
        else:
            def body(c, carry, s=s):
                chunk(s, c)
                return carry
            lax.fori_loop(0, NC, body, 0, unroll=4)

    @pl.when(t == NT - 1)
    def _():
        for s in range(SB):
            for h in range(GLA_H):
                sout_ref[s, h] = st_scr[s, h].T


def _gla(p1, p2, s0, wa, ba, nw, *, B, L):
    C = math.gcd(L, GLA_CHUNK)
    if L >= 512:
        SB, T = 1, 512
    else:
        SB, T = _pick(B, 8, 1), L
    NT = L // T
    R = SB * T
    kern = functools.partial(_gla_kernel, SB=SB, T=T, C=C, NT=NT)
    return pl.pallas_call(
        kern,
        grid=(B // SB, NT),
        in_specs=[pl.BlockSpec((R, 512), lambda g, t: (g * NT + t, 0)),
                  pl.BlockSpec((R, 512), lambda g, t: (g * NT + t, 1)),
                  pl.BlockSpec((R, D), lambda g, t: (g * NT + t, 1)),
                  pl.BlockSpec((R, D), lambda g, t: (g * NT + t, 2)),
                  pl.BlockSpec((R, TAIL_COLS), lambda g, t: (g * NT + t, 0)),
                  pl.BlockSpec((SB, GLA_H, GLA_DK, GLA_DV), lambda g, t: (g, 0, 0, 0)),
                  pl.BlockSpec((TAIL_COLS, GLA_H * GLA_DK), lambda g, t: (0, 0)),
                  pl.BlockSpec((1, GLA_H * GLA_DK), lambda g, t: (0, 0)),
                  pl.BlockSpec((1, GLA_DV), lambda g, t: (0, 0))],
        out_specs=[pl.BlockSpec((R, D), lambda g, t: (g * NT + t, 0)),
                   pl.BlockSpec((SB, GLA_H, GLA_DK, GLA_DV), lambda g, t: (g, 0, 0, 0))],
        out_shape=[jax.ShapeDtypeStruct((B * L, D), BF16 if C % 16 == 0 else F32),
                   jax.ShapeDtypeStruct((B, GLA_H, GLA_DK, GLA_DV), F32)],
        scratch_shapes=[pltpu.VMEM((SB, GLA_H, GLA_DV, GLA_DK), F32),
                        pltpu.VMEM((R, GLA_H * GLA_DK), F32)],
        compiler_params=_cparams(("parallel", "arbitrary")),
        name="gla",
    )(p1, p1, p1, p1, p2, s0, wa, ba, nw)


def _rope128(x, cos, sin, lo):
    rot = jnp.where(lo, pltpu.roll(x, 96, 1), pltpu.roll(x, 32, 1))
    return x * cos + rot * sin


def _q_kernel(mq_ref, anw_ref, w_ref, qnw_ref, cos_ref, sin_ref, q_ref):
    xn = _rms(mq_ref[...], anw_ref[...]).astype(BF16)
    q = _dot(xn, w_ref[...])
    lo = lax.broadcasted_iota(jnp.int32, (1, LANES), 1) < 32
    cos = cos_ref[...]
    sin = sin_ref[...]
    wn = qnw_ref[:, 0:128] * Q_SCALE
    wr = qnw_ref[:, 128:256] * Q_SCALE
    for h in range(MLA_H):
        qn = q[:, h * HEAD_PAD:h * HEAD_PAD + 128]
        qr = q[:, h * HEAD_PAD + 128:(h + 1) * HEAD_PAD]
        ssq = jnp.sum(qn * qn + qr * qr, axis=-1, keepdims=True)
        rinv = lax.rsqrt(ssq / MLA_DQK + EPS)
        qn = qn * rinv * wn
        qr = _rope128(qr * rinv * wr, cos, sin, lo)
        q_ref[:, h * HEAD_PAD:h * HEAD_PAD + 128] = qn.astype(BF16)
        q_ref[:, h * HEAD_PAD + 128:(h + 1) * HEAD_PAD] = qr.astype(BF16)


def _qpath(p1, anw, w_uq, qnw, cos, sin):
    M = p1.shape[0]
    tr = cos.shape[0]
    tm = _pick(math.gcd(M, tr), 512)
    nper = tr // tm
    return pl.pallas_call(
        _q_kernel,
        grid=(M // tm,),
        in_specs=[pl.BlockSpec((tm, Q_LORA), lambda i: (i, 4)),
                  pl.BlockSpec((1, Q_LORA), lambda i: (0, 0)),
                  pl.BlockSpec((Q_LORA, MLA_H * HEAD_PAD), lambda i: (0, 0)),
                  pl.BlockSpec((1, HEAD_PAD), lambda i: (0, 0)),
                  pl.BlockSpec((tm, LANES), lambda i: (i % nper, 0)),
                  pl.BlockSpec((tm, LANES), lambda i: (i % nper, 0))],
        out_specs=pl.BlockSpec((tm, MLA_H * HEAD_PAD), lambda i: (i, 0)),
        out_shape=jax.ShapeDtypeStruct((M, MLA_H * HEAD_PAD), BF16),
        compiler_params=_cparams(("parallel",)),
        name="qpath",
    )(p1, anw, w_uq, qnw, cos, sin)


def _kv_kernel(mkv_ref, tail_ref, anw_ref, wuk_ref, wuv_ref, kw_ref, cos_ref, sin_ref, c_ref, *kv_refs):
    c = _rms(mkv_ref[...], anw_ref[...])
    c_ref[...] = c
    if not kv_refs:
        return
    k_ref, v_ref = kv_refs
    cb = c.astype(BF16)
    kn = _dot(cb, wuk_ref[...])
    v_ref[...] = _dot(cb, wuv_ref[...]).astype(BF16)
    lane = lax.broadcasted_iota(jnp.int32, (1, LANES), 1)
    pe = jnp.where(lane < MLA_ROPE, tail_ref[...], 0.0)
    ssq_pe = jnp.sum(pe * pe, axis=-1, keepdims=True)
    kr = _rope128(pe * kw_ref[:, 128:256], cos_ref[...], sin_ref[...], lane < 32)
    wn = kw_ref[:, 0:128]
    for h in range(MLA_H):
        knh = kn[:, h * 128:(h + 1) * 128]
        rinv = lax.rsqrt((jnp.sum(knh * knh, axis=-1, keepdims=True) + ssq_pe) / MLA_DQK + EPS)
        k_ref[:, h * HEAD_PAD:h * HEAD_PAD + 128] = (knh * rinv * wn).astype(BF16)
        k_ref[:, h * HEAD_PAD + 128:(h + 1) * HEAD_PAD] = (kr * rinv).astype(BF16)


def _kvprep(p1, p2, anw, wuk, wuv, kw, cos, sin, *, with_kv):
    M = p1.shape[0]
    tr = cos.shape[0]
    tm = _pick(math.gcd(M, tr), 512)
    nper = tr // tm
    out_specs = [pl.BlockSpec((tm, KV_LORA), lambda i: (i, 0))]
    out_shape = [jax.ShapeDtypeStruct((M, KV_LORA), F32)]
    if with_kv:
        out_specs += [pl.BlockSpec((tm, MLA_H * HEAD_PAD), lambda i: (i, 0)),
                      pl.BlockSpec((tm, MLA_H * MLA_DV), lambda i: (i, 0))]
        out_shape += [jax.ShapeDtypeStruct((M, MLA_H * HEAD_PAD), BF16),
                      jax.ShapeDtypeStruct((M, MLA_H * MLA_DV), BF16)]
    return pl.pallas_call(
        _kv_kernel,
        grid=(M // tm,),
        in_specs=[pl.BlockSpec((tm, KV_LORA), lambda i: (i, 5)),
                  pl.BlockSpec((tm, TAIL_COLS), lambda i: (i, 0)),
                  pl.BlockSpec((1, KV_LORA), lambda i: (0, 0)),
                  pl.BlockSpec((KV_LORA, MLA_H * MLA_NOPE), lambda i: (0, 0)),
                  pl.BlockSpec((KV_LORA, MLA_H * MLA_DV), lambda i: (0, 0)),
                  pl.BlockSpec((1, HEAD_PAD), lambda i: (0, 0)),
                  pl.BlockSpec((tm, LANES), lambda i: (i % nper, 0)),
                  pl.BlockSpec((tm, LANES), lambda i: (i % nper, 0))],
        out_specs=out_specs,
        out_shape=out_shape,
        compiler_params=_cparams(("parallel",)),
        name="kvprep",
    )(p1, p2, anw, wuk, wuv, kw, cos, sin)


def _attn_kernel(q_ref, k_ref, v_ref, o_ref, s_scr, m_scr, l_scr, acc_scr, *, TQ, TK):
    R = TQ // TK
    HL = TK // LANES
    nfull = pl.program_id(2) * R

    def fold(x, op):
        y = x[:, 0:LANES]
        for g in range(1, HL):
            y = op(y, x[:, g * LANES:(g + 1) * LANES])
        return y

    def kv_rows(j):
        return pl.ds(pl.multiple_of(j * TK, TK), TK)

    m_scr[...] = jnp.full((TQ, LANES), -jnp.inf, F32)

    U = 2 if R % 2 == 0 else 1

    def full_a(jj, carry):
        m = m_scr[...]
        for u in range(U):
            j = jj * U + u
            s = _dot_nt(q_ref[...], k_ref[kv_rows(j), :])
            s_scr[j] = s
            m = jnp.maximum(m, fold(s, jnp.maximum))
        m_scr[...] = m
        return carry

    lax.fori_loop(0, nfull // U, full_a, 0)
    tri = lax.broadcasted_iota(jnp.int32, (TK, TK), 1) <= lax.broadcasted_iota(jnp.int32, (TK, TK), 0)
    for d in range(R):
        r0 = d * TK
        j = nfull + d
        s = _dot_nt(q_ref[r0:, :], k_ref[kv_rows(j), :])
        sd = jnp.where(tri, s[0:TK], -jnp.inf)
        s_scr[j, r0:r0 + TK, :] = sd
        m_scr[r0:r0 + TK, :] = jnp.maximum(m_scr[r0:r0 + TK, :], fold(sd, jnp.maximum))
        if r0 + TK < TQ:
            s_scr[j, r0 + TK:, :] = s[TK:]
            m_scr[r0 + TK:, :] = jnp.maximum(m_scr[r0 + TK:, :], fold(s[TK:], jnp.maximum))

    m_scr[...] = jnp.broadcast_to(jnp.max(m_scr[...], axis=-1, keepdims=True), (TQ, LANES))
    l_scr[...] = jnp.zeros((TQ, LANES), F32)
    acc_scr[...] = jnp.zeros((TQ, MLA_DV), F32)

    def probs(s, m):
        return jnp.concatenate([jnp.exp2(s[:, g * LANES:(g + 1) * LANES] - m) for g in range(HL)], axis=1)

    def full_b(jj, carry):
        ps = [probs(s_scr[jj * U + u], m_scr[...]) for u in range(U)]
        l_scr[...] += functools.reduce(jnp.add, [fold(p, jnp.add) for p in ps])
        pv = jnp.concatenate(ps, axis=1).astype(BF16)
        acc_scr[...] += _dot(pv, v_ref[pl.ds(pl.multiple_of(jj * (U * TK), U * TK), U * TK), :])
        return carry

    lax.fori_loop(0, nfull // U, full_b, 0)
    for d in range(R):
        r0 = d * TK
        j = nfull + d
        p = probs(s_scr[j, r0:, :], m_scr[r0:, :])
        l_scr[r0:, :] += fold(p, jnp.add)
        acc_scr[r0:, :] += _dot(p.astype(BF16), v_ref[kv_rows(j), :])
    o_ref[...] = (acc_scr[...] / jnp.sum(l_scr[...], axis=-1, keepdims=True)).astype(o_ref.dtype)


def _attn(q, k, v, *, B, L):
    TK = 256
    TQ = _pick(L, 1024, TK)
    NQ = L // TQ
    kern = functools.partial(_attn_kernel, TQ=TQ, TK=TK)
    return pl.pallas_call(
        kern,
        grid=(B, MLA_H, NQ),
        in_specs=[pl.BlockSpec((TQ, HEAD_PAD), lambda b, h, i: (b * NQ + i, h)),
                  pl.BlockSpec((L, HEAD_PAD), lambda b, h, i: (b, h)),
                  pl.BlockSpec((L, MLA_DV), lambda b, h, i: (b, h))],
        out_specs=pl.BlockSpec((TQ, MLA_DV), lambda b, h, i: (b * NQ + i, h)),
        out_shape=jax.ShapeDtypeStruct((B * L, MLA_H * MLA_DV), BF16),
        scratch_shapes=[pltpu.VMEM((L // TK, TQ, TK), F32), pltpu.VMEM((TQ, LANES), F32),
                        pltpu.VMEM((TQ, LANES), F32), pltpu.VMEM((TQ, MLA_DV), F32)],
        compiler_params=_cparams(("parallel", "parallel", "arbitrary")),
        name="attn",
    )(q, k, v)


def _qs_kernel(q_ref, wukT_ref, kw_ref, qlat_ref, a_ref, *, DB, DL):
    q = q_ref[...].astype(F32)
    qn = (q[:, 0:128] * kw_ref[:, 0:128]).astype(BF16)
    qlat_ref[...] = _dot(qn, wukT_ref[...]).reshape(DB, 1, DL, KV_LORA)
    qr = q[:, 128:256]
    wr = kw_ref[:, 128:256]
    lo = lax.broadcasted_iota(jnp.int32, (1, LANES), 1) < 32
    a1 = qr * wr
    a2 = jnp.where(lo, pltpu.roll(qr, 96, 1), -pltpu.roll(qr, 32, 1)) * wr
    a_ref[...] = (a1 + pltpu.roll(a2, 64, 1)).reshape(DB, 1, DL, LANES)


def _qs(q_s, wukT, kw, *, DB, DL):
    kern = functools.partial(_qs_kernel, DB=DB, DL=DL)
    return pl.pallas_call(
        kern,
        grid=(MLA_H,),
        in_specs=[pl.BlockSpec((DB * DL, HEAD_PAD), lambda h: (0, h)),
                  pl.BlockSpec((MLA_NOPE, KV_LORA), lambda h: (h, 0)),
                  pl.BlockSpec((1, HEAD_PAD), lambda h: (0, 0))],
        out_specs=[pl.BlockSpec((DB, 1, DL, KV_LORA), lambda h: (0, h, 0, 0)),
                   pl.BlockSpec((DB, 1, DL, LANES), lambda h: (0, h, 0, 0))],
        out_shape=[jax.ShapeDtypeStruct((DB, MLA_H, DL, KV_LORA), F32),
                   jax.ShapeDtypeStruct((DB, MLA_H, DL, LANES), F32)],
        compiler_params=_cparams(("parallel",)),
        name="qs",
    )(q_s, wukT, kw)


def _decode_kernel(pt_ref, *refs, PPS, NS, DL, CP):
    lat_refs = refs[:PPS]
    pe_refs = refs[PPS:2 * PPS]
    (cst_ref, csn_ref, qlat_ref, a_ref, wukT_ref, cnew_ref, tnew_ref, wuv_ref, y_ref,
     wall_scr, cb_scr, k2t_scr, m_scr, l_scr, acc_scr) = refs[2 * PPS:]
    del pt_ref
    b = pl.program_id(0)
    s = pl.program_id(1)
    NR = MLA_H * DL
    CK = CP * PAGE

    @pl.when((b == 0) & (s == 0))
    def _():
        wall_scr[0:MLA_H * MLA_NOPE, :] = wukT_ref[...]

    @pl.when(s == 0)
    def _():
        wall_scr[MLA_H * MLA_NOPE:, :] = qlat_ref[...].reshape(NR, KV_LORA).astype(BF16)
        m_scr[...] = jnp.full((NR, 1), -jnp.inf, F32)
        l_scr[...] = jnp.zeros((NR, 1), F32)
        acc_scr[...] = jnp.zeros((NR, KV_LORA), F32)

    a = a_ref[...].reshape(NR, LANES).astype(BF16)

    def scores(cb, s_rope, ssq_pe):
        n = cb.shape[0]
        big = _dot_nt(wall_scr[...], cb)
        kn = big[0:MLA_H * MLA_NOPE, :].reshape(MLA_H, MLA_NOPE, n)
        rinv = lax.rsqrt((jnp.sum(kn * kn, axis=1) + ssq_pe) / MLA_DQK + EPS)
        sc = big[MLA_H * MLA_NOPE:, :] + s_rope
        return (sc.reshape(MLA_H, DL, n) * rinv[:, None, :]).reshape(NR, n)

    def update(sc, vals):
        m_old = m_scr[...]
        m_new = jnp.maximum(m_old, jnp.max(sc, axis=-1, keepdims=True))
        alpha = jnp.exp2(m_old - m_new)
        p = jnp.exp2(sc - m_new)
        l_scr[...] = l_scr[...] * alpha + jnp.sum(p, axis=-1, keepdims=True)
        acc_scr[...] = acc_scr[...] * alpha + _dot(p.astype(BF16), vals)
        m_scr[...] = m_new

    @pl.when(s < NS)
    def _():
        chunks = []
        for c in range(PPS // CP):
            sq = []
            for i in range(c * CP, (c + 1) * CP):
                keys = slice(i * PAGE, (i + 1) * PAGE)
                cb_scr[keys, :] = lat_refs[i][...].astype(BF16)
                pet = pe_refs[i][...]
                k2t_scr[0:MLA_ROPE, keys] = (pet * cst_ref[0:MLA_ROPE, keys]).astype(BF16)
                k2t_scr[MLA_ROPE:, keys] = (pet * cst_ref[MLA_ROPE:, keys]).astype(BF16)
                sq.append(jnp.sum(pet * pet, axis=0, keepdims=True))
            ck = slice(c * CK, (c + 1) * CK)
            s_rope = _dot(a, k2t_scr[:, ck])
            chunks.append((scores(cb_scr[ck, :], s_rope, jnp.concatenate(sq, axis=1)), ck))
        for sc, ck in chunks:
            update(sc, cb_scr[ck, :])

    @pl.when(s == NS)
    def _():
        lane = lax.broadcasted_iota(jnp.int32, (1, LANES), 1)
        pad = lambda x: jnp.concatenate([x, jnp.zeros((PAGE - DL, x.shape[1]), F32)], axis=0)
        cn = pad(cnew_ref[...]).astype(BF16)
        pe = jnp.where(lane < MLA_ROPE, tnew_ref[...], 0.0)
        k2 = pad((pe + pltpu.roll(pe, 64, 1)) * csn_ref[...]).astype(BF16)
        sq = pad(pe * pe)
        sqh = sq.astype(BF16)
        sql = (sq - sqh.astype(F32)).astype(BF16)
        ones = jnp.ones((MLA_H, LANES), BF16)
        sc = scores(cn, _dot_nt(a, k2), _dot_nt(ones, sqh) + _dot_nt(ones, sql))
        qidx = lax.broadcasted_iota(jnp.int32, (MLA_H, DL, PAGE), 1).reshape(NR, PAGE)
        kidx = lax.broadcasted_iota(jnp.int32, (NR, PAGE), 1)
        update(jnp.where(kidx <= qidx, sc, -jnp.inf), cn)
        o = (acc_scr[...] / l_scr[...]).astype(BF16)
        full = _dot(o, wuv_ref[...])
        for h in range(MLA_H):
            y_ref[:, h * MLA_DV:(h + 1) * MLA_DV] = full[h * DL:(h + 1) * DL, h * MLA_DV:(h + 1) * MLA_DV]


def _decode(page_table, cache_lat, cache_pet, layer, cst, csn, qlat, a, wukT, c_new, tail_new, wuv, *, DB, DL, PPS):
    n_pages = page_table.shape[1]
    NS = n_pages // PPS
    TK = PPS * PAGE
    NR = MLA_H * DL
    CP = math.gcd(PPS, 4)

    pt3 = page_table.reshape(DB, NS, PPS)
    page_table = jnp.concatenate([pt3, pt3[:, -1:]], axis=1).reshape(-1)

    def page_map(i):
        return lambda b, s, pt: (layer, pt[(b * (NS + 1) + s) * PPS + i], 0, 0)

    in_specs = ([pl.BlockSpec((None, None, PAGE, KV_LORA), page_map(i)) for i in range(PPS)]
                + [pl.BlockSpec((None, None, MLA_ROPE, PAGE), page_map(i)) for i in range(PPS)]
                + [pl.BlockSpec((LANES, TK), lambda b, s, pt: (0, jnp.minimum(s, NS - 1))),
                   pl.BlockSpec((DL, LANES), lambda b, s, pt: (0, 0)),
                   pl.BlockSpec((None, MLA_H, DL, KV_LORA), lambda b, s, pt: (b, 0, 0, 0)),
                   pl.BlockSpec((None, MLA_H, DL, LANES), lambda b, s, pt: (b, 0, 0, 0)),
                   pl.BlockSpec((MLA_H * MLA_NOPE, KV_LORA), lambda b, s, pt: (0, 0)),
                   pl.BlockSpec((DL, KV_LORA), lambda b, s, pt: (b, 0)),
                   pl.BlockSpec((DL, TAIL_COLS), lambda b, s, pt: (b, 0)),
                   pl.BlockSpec((KV_LORA, MLA_H * MLA_DV), lambda b, s, pt: (0, 0))])
    grid_spec = pltpu.PrefetchScalarGridSpec(
        num_scalar_prefetch=1,
        grid=(DB, NS + 1),
        in_specs=in_specs,
        out_specs=pl.BlockSpec((DL, MLA_H * MLA_DV), lambda b, s, pt: (b, 0)),
        scratch_shapes=[pltpu.VMEM((MLA_H * MLA_NOPE + NR, KV_LORA), BF16),
                        pltpu.VMEM((TK, KV_LORA), BF16),
                        pltpu.VMEM((LANES, TK), BF16),
                        pltpu.VMEM((NR, 1), F32),
                        pltpu.VMEM((NR, 1), F32),
                        pltpu.VMEM((NR, KV_LORA), F32)])
    kern = functools.partial(_decode_kernel, PPS=PPS, NS=NS, DL=DL, CP=CP)
    return pl.pallas_call(
        kern,
        grid_spec=grid_spec,
        out_shape=jax.ShapeDtypeStruct((DB * DL, MLA_H * MLA_DV), F32),
        compiler_params=_cparams(("arbitrary", "arbitrary")),
        name="decode",
    )(page_table, *([cache_lat] * PPS), *([cache_pet] * PPS), cst, csn, qlat, a, wukT, c_new, tail_new, wuv)


def _merge_kernel(g0_ref, g1_ref, g2_ref, bg_ref, yp_ref, yg_ref, ym_ref, x_ref, w_ref, o_ref):
    def gate(g_ref, i):
        return 1.0 / (1.0 + jnp.exp(-(g_ref[...].astype(F32) + bg_ref[i:i + 1, :])))

    merged = gate(g0_ref, 0) * yp_ref[...] + gate(g1_ref, 1) * yg_ref[...] + gate(g2_ref, 2) * ym_ref[...]
    o_ref[...] = x_ref[...] + _dot(merged.astype(BF16), w_ref[...])


def _merge(p1, bg, yp, yg, ym, x, w_out):
    M = x.shape[0]
    tm = _pick(M, 512)
    row = lambda i: (i, 0)
    return pl.pallas_call(
        _merge_kernel,
        grid=(M // tm,),
        in_specs=[pl.BlockSpec((tm, D), lambda i: (i, 3)),
                  pl.BlockSpec((tm, D), lambda i: (i, 4)),
                  pl.BlockSpec((tm, D), lambda i: (i, 5)),
                  pl.BlockSpec((3, D), lambda i: (0, 0)),
                  pl.BlockSpec((tm, D), row), pl.BlockSpec((tm, D), row), pl.BlockSpec((tm, D), row),
                  pl.BlockSpec((tm, D), row),
                  pl.BlockSpec((D, D), lambda i: (0, 0))],
        out_specs=pl.BlockSpec((tm, D), row),
        out_shape=jax.ShapeDtypeStruct((M, D), F32),
        compiler_params=_cparams(("parallel",)),
        name="merge",
    )(p1, p1, p1, bg, yp, yg, ym, x, w_out)


def _mlp_kernel(x_ref, nw_ref, wu_ref, wd_ref, o_ref, h_scr, acc_scr):
    f = pl.program_id(1)

    @pl.when(f == 0)
    def _():
        h_scr[...] = _rms(x_ref[...], nw_ref[...]).astype(BF16)
        acc_scr[...] = jnp.zeros_like(acc_scr)

    a = jnp.maximum(_dot(h_scr[...], wu_ref[...]), 0.0)
    acc_scr[...] += _dot((a * a).astype(BF16), wd_ref[...])

    @pl.when(f == pl.num_programs(1) - 1)
    def _():
        o_ref[...] = x_ref[...] + acc_scr[...]


def _mlp(x, nw, w_up, w_down):
    M = x.shape[0]
    tm = _pick(M, 1024)
    tf = 512
    return pl.pallas_call(
        _mlp_kernel,
        grid=(M // tm, D_FF // tf),
        in_specs=[pl.BlockSpec((tm, D), lambda i, f: (i, 0)),
                  pl.BlockSpec((1, D), lambda i, f: (0, 0)),
                  pl.BlockSpec((D, tf), lambda i, f: (0, f)),
                  pl.BlockSpec((tf, D), lambda i, f: (f, 0))],
        out_specs=pl.BlockSpec((tm, D), lambda i, f: (i, 0)),
        out_shape=jax.ShapeDtypeStruct((M, D), F32),
        scratch_shapes=[pltpu.VMEM((tm, D), BF16), pltpu.VMEM((tm, D), F32)],
        compiler_params=_cparams(("parallel", "arbitrary")),
        name="mlp",
    )(x, nw, w_up, w_down)


def _rope_angles(pos):
    inv = 1.0 / (ROPE_THETA ** (jnp.arange(0, MLA_ROPE, 2, dtype=F32) / MLA_ROPE))
    return pos.astype(F32)[:, None] * inv[None, :]


def _token_tables(pos):
    ang = _rope_angles(pos)
    c, s = jnp.cos(ang), jnp.sin(ang)
    z = jnp.zeros((pos.shape[0], LANES - MLA_ROPE), F32)
    return jnp.concatenate([c, c, z], axis=1), jnp.concatenate([-s, s, z], axis=1)


def _key_table(pos):
    ang = _rope_angles(pos)
    c, s = jnp.cos(ang), jnp.sin(ang)
    return jnp.concatenate([c, c, s, s], axis=1)


def _pad_heads(w):
    lead = w.shape[:-1]
    w = w.reshape(lead + (MLA_H, MLA_DQK))
    w = jnp.concatenate([w, jnp.zeros(lead + (MLA_H, HEAD_PAD - MLA_DQK), w.dtype)], axis=-1)
    return w.reshape(lead + (MLA_H * HEAD_PAD,))


def kernel(x_prompt, x_sample, state_pool, state_gla, cache_kv_latent, cache_k_rope, page_table,
           norm1_w, w_in, b_gate, w_pool, pool_scale, w_gla_a2, b_gla_a, gla_norm_w,
           q_a_norm_w, kv_a_norm_w, w_uq, q_norm_w, w_uk, w_uv, k_norm_w, w_out,
           norm2_w, w_up, w_down):
    return _forward(x_prompt, x_sample, state_pool, state_gla, cache_kv_latent, cache_k_rope, page_table,
                    norm1_w, w_in, b_gate, w_pool, pool_scale, w_gla_a2, b_gla_a, gla_norm_w,
                    q_a_norm_w, kv_a_norm_w, w_uq, q_norm_w, w_uk, w_uv, k_norm_w, w_out,
                    norm2_w, w_up, w_down, pages_per_step=16)


def _forward(x_prompt, x_sample, state_pool, state_gla, cache_kv_latent, cache_k_rope, page_table,
             norm1_w, w_in, b_gate, w_pool, pool_scale, w_gla_a2, b_gla_a, gla_norm_w,
             q_a_norm_w, kv_a_norm_w, w_uq, q_norm_w, w_uk, w_uv, k_norm_w, w_out,
             norm2_w, w_up, w_down, *, pages_per_step):
    B, L, _ = x_prompt.shape
    DB, DL, _ = x_sample.shape
    depth = w_in.shape[0]
    n_pages = page_table.shape[1]
    past = n_pages * PAGE
    PPS = pages_per_step
    assert n_pages % PPS == 0

    o = [0]
    for n in (1024, 512, 512, 1024, GLA_RANK, 1024, Q_LORA, KV_LORA, MLA_ROPE, 3 * D):
        o.append(o[-1] + n)
    seg = lambda i: w_in[:, :, o[i]:o[i + 1]]
    w_main = jnp.concatenate([seg(0), seg(6), seg(7), seg(1), seg(2), seg(3), seg(5), seg(9)], axis=-1).astype(BF16)
    w_tail = jnp.concatenate([seg(8), seg(4), jnp.zeros((depth, D, TAIL_COLS - MLA_ROPE - GLA_RANK), F32)],
                             axis=-1).astype(BF16)
    wa_pad = jnp.zeros((depth, TAIL_COLS, GLA_H * GLA_DK), F32).at[:, MLA_ROPE:MLA_ROPE + GLA_RANK, :].set(w_gla_a2)
    wa_pad = wa_pad.astype(BF16)
    w_pool_b = w_pool.astype(BF16)
    w_uq_p = _pad_heads(w_uq).astype(BF16)
    qnw_p = jnp.concatenate([q_norm_w, jnp.zeros((depth, HEAD_PAD - MLA_DQK), F32)], axis=-1)[:, None, :]
    knw_p = jnp.concatenate([k_norm_w, jnp.zeros((depth, HEAD_PAD - MLA_DQK), F32)], axis=-1)[:, None, :]
    w_uk_b = w_uk.astype(BF16)
    w_ukT_b = jnp.swapaxes(w_uk, 1, 2).astype(BF16)
    w_uv_b = w_uv.astype(BF16)
    w_out_b = w_out.astype(BF16)
    w_up_b = w_up.astype(BF16)
    w_down_b = w_down.astype(BF16)

    cos_p, sin_p = _token_tables(jnp.arange(L))
    reps = math.gcd(DB, 512 // DL)
    cos_s, sin_s = _token_tables(jnp.tile(past + jnp.arange(DL), reps))
    cs_past_t = _key_table(jnp.arange(past)).T
    cs_new = _key_table(past + jnp.arange(DL))
    cache_pet = jnp.swapaxes(cache_k_rope, 2, 3)

    xp = x_prompt.reshape(B * L, D)
    xs = x_sample.reshape(DB * DL, D)
    zero_prefix = jnp.zeros((B, POOL_STATE, D), F32)
    zero_state = jnp.zeros((B, GLA_H, GLA_DK, GLA_DV), F32)

    outs = [[] for _ in range(8)]
    for l in range(depth):
        r1 = lambda w: w[l][None, :]
        p1, pb, p2 = _proj(xp, r1(norm1_w), w_main[l], w_tail[l])
        yp, pool_p = _pool(p1, zero_prefix, w_pool_b[l], r1(pool_scale), B=B, L=L, start_pos=0)
        yg, gla_p = _gla(pb, p2, zero_state, wa_pad[l], r1(b_gla_a), r1(gla_norm_w), B=B, L=L)
        q = _qpath(p1, r1(q_a_norm_w), w_uq_p[l], qnw_p[l], cos_p, sin_p)
        c_p, k, v = _kvprep(p1, p2, r1(kv_a_norm_w), w_uk_b[l], w_uv_b[l], knw_p[l], cos_p, sin_p, with_kv=True)
        ym = _attn(q, k, v, B=B, L=L)
        xp = _merge(pb, b_gate[l], yp, yg, ym, xp, w_out_b[l])
        xp = _mlp(xp, r1(norm2_w), w_up_b[l], w_down_b[l])
        pe_p = p2[:, :MLA_ROPE]
        s1, sb, s2 = _proj(xs, r1(norm1_w), w_main[l], w_tail[l])
        yp, pool_s = _pool(s1, state_pool[l], w_pool_b[l], r1(pool_scale), B=DB, L=DL, start_pos=past)
        yg, gla_s = _gla(sb, s2, state_gla[l], wa_pad[l], r1(b_gla_a), r1(gla_norm_w), B=DB, L=DL)
        q = _qpath(s1, r1(q_a_norm_w), w_uq_p[l], qnw_p[l], cos_s, sin_s)
        (c_s,) = _kvprep(s1, s2, r1(kv_a_norm_w), w_uk_b[l], w_uv_b[l], knw_p[l], cos_s, sin_s, with_kv=False)
        qlat, a = _qs(q, w_ukT_b[l], knw_p[l], DB=DB, DL=DL)
        ym = _decode(page_table, cache_kv_latent, cache_pet, l, cs_past_t, cs_new, qlat, a, w_ukT_b[l], c_s, s2,
                     w_uv_b[l], DB=DB, DL=DL, PPS=PPS)
        xs = _merge(sb, b_gate[l], yp, yg, ym, xs, w_out_b[l])
        xs = _mlp(xs, r1(norm2_w), w_up_b[l], w_down_b[l])
        pe_s = s2[:, :MLA_ROPE]
        for lst, val in zip(outs, (pool_p, pool_s, gla_p, gla_s,
                                   c_p.reshape(B, L, KV_LORA), c_s.reshape(DB, DL, KV_LORA),
                                   pe_p.reshape(B, L, MLA_ROPE), pe_s.reshape(DB, DL, MLA_ROPE))):
            lst.append(val)
    return (xp.reshape(B, L, D), xs.reshape(DB, DL, D)) + tuple(jnp.stack(v) for v in outs)
```

```python
import functools
import math

import jax
import jax.numpy as jnp
from jax import lax
from jax.experimental import pallas as pl
from jax.experimental.pallas import tpu as pltpu

F32 = jnp.float32
BF16 = jnp.bfloat16
EPS = 1e-6

D = 1024
POOL_WINDOWS = (2, 4, 8, 16)
POOL_GW = 256
POOL_STATE = 15
POOL_HDR = 32
GLA_H = 4
GLA_DK = 128
GLA_DV = 256
GLA_RANK = 16
GLA_TAU = 16.0
GLA_CHUNK = 64
MLA_H = 8
MLA_NOPE = 128
MLA_ROPE = 64
MLA_DQK = MLA_NOPE + MLA_ROPE
MLA_DV = 128
Q_LORA = 256
KV_LORA = 256
ROPE_THETA = 10000.0
ATTN_SCALE = MLA_DQK ** -0.5
Q_SCALE = ATTN_SCALE * math.log2(math.e)
D_FF = 4 * D
PAGE = 128
HEAD_PAD = 256
LANES = 128

F32_COLS = 1536
B16_COLS = 6144
TAIL_COLS = 128

NT_DIMS = (((1,), (1,)), ((), ()))
TN_DIMS = (((0,), (0,)), ((), ()))


def _cparams(sem, vmem_mb=48):
    return pltpu.CompilerParams(dimension_semantics=sem, vmem_limit_bytes=vmem_mb * 1024 * 1024)


def _dot(a, b):
    return jnp.dot(a, b, preferred_element_type=F32)


def _dot_nt(a, b):
    return lax.dot_general(a, b, NT_DIMS, preferred_element_type=F32)


def _dot_tn(a, b):
    return lax.dot_general(a, b, TN_DIMS, preferred_element_type=F32)


def _rms(x, w):
    return x * lax.rsqrt(jnp.mean(x * x, axis=-1, keepdims=True) + EPS) * w


def _pick(n, target, mult=16):
    best = None
    for t in range(mult, min(n, target) + 1, mult):
        if n % t == 0:
            best = t
    assert best is not None, (n, target)
    return best


def _proj_kernel(x_ref, nw_ref, w_ref, wt_ref, of_ref, ob_ref, ot_ref, h_scr, *, NF):
    j = pl.program_id(1)

    @pl.when(j == 0)
    def _():
        h = _rms(x_ref[...], nw_ref[...]).astype(BF16)
        h_scr[...] = h
        ot_ref[...] = _dot(h, wt_ref[...])

    r = _dot(h_scr[...], w_ref[...])

    @pl.when(j < NF)
    def _():
        of_ref[...] = r

    @pl.when(j >= NF)
    def _():
        ob_ref[...] = r.astype(BF16)


def _proj(x, nw, w_main, w_tail):
    M = x.shape[0]
    tm = _pick(M, 1024)
    tn = 512
    NF = F32_COLS // tn
    return pl.pallas_call(
        functools.partial(_proj_kernel, NF=NF),
        grid=(M // tm, (F32_COLS + B16_COLS) // tn),
        in_specs=[pl.BlockSpec((tm, D), lambda i, j: (i, 0)),
                  pl.BlockSpec((1, D), lambda i, j: (0, 0)),
                  pl.BlockSpec((D, tn), lambda i, j: (0, j)),
                  pl.BlockSpec((D, TAIL_COLS), lambda i, j: (0, 0))],
        out_specs=[pl.BlockSpec((tm, tn), lambda i, j: (i, jnp.minimum(j, NF - 1))),
                   pl.BlockSpec((tm, tn), lambda i, j: (i, jnp.maximum(j - NF, 0))),
                   pl.BlockSpec((tm, TAIL_COLS), lambda i, j: (i, 0))],
        out_shape=[jax.ShapeDtypeStruct((M, F32_COLS), F32),
                   jax.ShapeDtypeStruct((M, B16_COLS), BF16),
                   jax.ShapeDtypeStruct((M, TAIL_COLS), F32)],
        scratch_shapes=[pltpu.VMEM((tm, D), BF16)],
        compiler_params=_cparams(("parallel", "arbitrary")),
        name="proj",
    )(x, nw, w_main, w_tail)


def _pool_kernel(u_ref, pre_ref, w_ref, sc_ref, y_ref, st_ref, ext_scr, lvl_scr, pooled_scr,
                 *, SB, T, NT, start_pos):
    t = pl.program_id(1)
    H, R0, GW = POOL_HDR, 16, POOL_GW
    pos = start_pos + t * T + lax.broadcasted_iota(jnp.int32, (T, 1), 0)

    @pl.when(t == 0)
    def _():
        lvl_scr[:, 0:R0, :] = jnp.zeros((3, R0, D), F32)

    for s in range(SB):
        @pl.when(t == 0)
        def _():
            ext_scr[s, 0:H - POOL_STATE, :] = jnp.zeros((H - POOL_STATE, D), F32)
            ext_scr[s, H - POOL_STATE:H, :] = pre_ref[s]

        ext_scr[s, H:H + T, :] = u_ref[s * T:(s + 1) * T, :]
        a1 = ext_scr[s, R0:H + T, :] + ext_scr[s, R0 - 1:H + T - 1, :]
        lvl_scr[0, R0:H + T, :] = a1
        a2 = a1[:, GW:] + lvl_scr[0, R0 - 2:H + T - 2, GW:]
        lvl_scr[1, R0:H + T, GW:] = a2
        a3 = a2[:, GW:] + lvl_scr[1, R0 - 4:H + T - 4, 2 * GW:]
        lvl_scr[2, R0:H + T, 2 * GW:] = a3
        a4 = a3[H - R0:, GW:] + lvl_scr[2, H - 8:H + T - 8, 3 * GW:]
        wsum = (a1[H - R0:, 0:GW], a2[H - R0:, 0:GW], a3[H - R0:, 0:GW], a4)
        for g, w in enumerate(POOL_WINDOWS):
            cols = slice(g * GW, (g + 1) * GW)
            cnt = jnp.minimum(pos + 1, w).astype(F32)
            pooled_scr[s * T:(s + 1) * T, cols] = (wsum[g] / cnt - ext_scr[s, H:H + T, cols]).astype(BF16)
        last = ext_scr[s, H + T - POOL_STATE:H + T, :]

        @pl.when(t == NT - 1)
        def _():
            st_ref[s] = last

        ext_scr[s, H - POOL_STATE:H, :] = last
    for g in range(len(POOL_WINDOWS)):
        cols = slice(g * POOL_GW, (g + 1) * POOL_GW)
        y_ref[:, cols] = (_dot(pooled_scr[:, cols], w_ref[g]) * sc_ref[:, cols]).astype(y_ref.dtype)


def _pool(p1, prefix, w_pool, scale, *, B, L, start_pos):
    if L >= 512:
        SB, T = 1, 512
    else:
        SB, T = _pick(B, 16, 1), L
    NT = L // T
    kern = functools.partial(_pool_kernel, SB=SB, T=T, NT=NT, start_pos=start_pos)
    return pl.pallas_call(
        kern,
        grid=(B // SB, NT),
        in_specs=[pl.BlockSpec((SB * T, D), lambda g, t: (g * NT + t, 0)),
                  pl.BlockSpec((SB, POOL_STATE, D), lambda g, t: (g, 0, 0)),
                  pl.BlockSpec((4, POOL_GW, POOL_GW), lambda g, t: (0, 0, 0)),
                  pl.BlockSpec((1, D), lambda g, t: (0, 0))],
        out_specs=[pl.BlockSpec((SB * T, D), lambda g, t: (g * NT + t, 0)),
                   pl.BlockSpec((SB, POOL_STATE, D), lambda g, t: (g, 0, 0))],
        out_shape=[jax.ShapeDtypeStruct((B * L, D), BF16),
                   jax.ShapeDtypeStruct((B, POOL_STATE, D), F32)],
        scratch_shapes=[pltpu.VMEM((SB, POOL_HDR + T, D), F32), pltpu.VMEM((3, POOL_HDR + T, D), F32),
                        pltpu.VMEM((SB * T, D), BF16)],
        compiler_params=_cparams(("parallel", "arbitrary")),
        name="pool",
    )(p1, prefix, w_pool, scale)


def _gla_kernel(q_ref, k_ref, v_ref, r_ref, tail_ref, s0_ref, wa_ref, ba_ref, nw_ref,
                o_ref, sout_ref, st_scr, la_scr, *, SB, T, C, NT):
    t = pl.program_id(1)
    z = _dot(tail_ref[...].astype(BF16), wa_ref[...]) + ba_ref[...]
    la_scr[...] = (jnp.minimum(z, 0.0) - jnp.log(1.0 + jnp.exp(-jnp.abs(z)))) * (1.0 / GLA_TAU)

    @pl.when(t == 0)
    def _():
        for s in range(SB):
            for h in range(GLA_H):
                st_scr[s, h] = s0_ref[s, h].T

    row = lax.broadcasted_iota(jnp.int32, (C, C), 0)
    col = lax.broadcasted_iota(jnp.int32, (C, C), 1)
    tril = row >= col
    tri = tril.astype(F32).astype(BF16)
    nw = nw_ref[...]
    NC = T // C
    pre = [x[...].astype(F32) for x in (q_ref, k_ref, v_ref, r_ref)] if NC == 1 else None

    def load(i, ref, rows, cols=slice(None)):
        return pre[i][rows, cols] if pre is not None else ref[rows, cols].astype(F32)

    def chunk(s, c):
        if isinstance(c, int):
            rows = slice(s * T + c * C, s * T + (c + 1) * C)
        else:
            rows = pl.ds(pl.multiple_of(s * T + c * C, C), C)
        la = la_scr[rows, :]
        la_hi = la.astype(BF16)
        la_lo = (la - la_hi.astype(F32)).astype(BF16)
        b = _dot(tri, la_hi) + _dot(tri, la_lo)
        bl = b[C - 1:C, :]
        k = load(1, k_ref, rows)
        qt_all = (load(0, q_ref, rows) * (GLA_DK ** -0.5) * jnp.exp(b)).astype(BF16)
        kt_all = (k * jnp.exp(-b)).astype(BF16)
        kd_all = (k * jnp.exp(bl - b)).astype(BF16)
        dec = jnp.exp(bl)
        for h in range(GLA_H):
            kc = slice(h * GLA_DK, (h + 1) * GLA_DK)
            vc = slice(h * GLA_DV, (h + 1) * GLA_DV)
            qt = qt_all[:, kc]
            vb = load(2, v_ref, rows, vc).astype(BF16)
            att = jnp.where(tril, _dot_nt(qt, kt_all[:, kc]), 0.0)
            st = st_scr[s, h]
            o = _dot(att.astype(BF16), vb) + _dot_nt(qt, st.astype(BF16))
            st_scr[s, h] = st * dec[:, kc] + _dot_tn(vb, kd_all[:, kc])
            r = load(3, r_ref, rows, vc)
            o_ref[rows, vc] = (_rms(o, nw) * (r * (1.0 / (1.0 + jnp.exp(-r))))).astype(o_ref.dtype)

    for s in range(SB):
        if NC == 1:
            chunk(s, 0)
        else:
            def body(c, carry, s=s):
                chunk(s, c)
                return carry
            lax.fori_loop(0, NC, body, 0, unroll=4)

    @pl.when(t == NT - 1)
    def _():
        for s in range(SB):
            for h in range(GLA_H):
                sout_ref[s, h] = st_scr[s, h].T


def _gla(p1, p2, s0, wa, ba, nw, *, B, L):
    C = math.gcd(L, GLA_CHUNK)
    if L >= 512:
        SB, T = 1, 512
    else:
        SB, T = _pick(B, 8, 1), L
    NT = L // T
    R = SB * T
    kern = functools.partial(_gla_kernel, SB=SB, T=T, C=C, NT=NT)
    return pl.pallas_call(
        kern,
        grid=(B // SB, NT),
        in_specs=[pl.BlockSpec((R, 512), lambda g, t: (g * NT + t, 0)),
                  pl.BlockSpec((R, 512), lambda g, t: (g * NT + t, 1)),
                  pl.BlockSpec((R, D), lambda g, t: (g * NT + t, 1)),
                  pl.BlockSpec((R, D), lambda g, t: (g * NT + t, 2)),
                  pl.BlockSpec((R, TAIL_COLS), lambda g, t: (g * NT + t, 0)),
                  pl.BlockSpec((SB, GLA_H, GLA_DK, GLA_DV), lambda g, t: (g, 0, 0, 0)),
                  pl.BlockSpec((TAIL_COLS, GLA_H * GLA_DK), lambda g, t: (0, 0)),
                  pl.BlockSpec((1, GLA_H * GLA_DK), lambda g, t: (0, 0)),
                  pl.BlockSpec((1, GLA_DV), lambda g, t: (0, 0))],
        out_specs=[pl.BlockSpec((R, D), lambda g, t: (g * NT + t, 0)),
                   pl.BlockSpec((SB, GLA_H, GLA_DK, GLA_DV), lambda g, t: (g, 0, 0, 0))],
        out_shape=[jax.ShapeDtypeStruct((B * L, D), BF16 if C % 16 == 0 else F32),
                   jax.ShapeDtypeStruct((B, GLA_H, GLA_DK, GLA_DV), F32)],
        scratch_shapes=[pltpu.VMEM((SB, GLA_H, GLA_DV, GLA_DK), F32),
                        pltpu.VMEM((R, GLA_H * GLA_DK), F32)],
        compiler_params=_cparams(("parallel", "arbitrary")),
        name="gla",
    )(p1, p1, p1, p1, p2, s0, wa, ba, nw)


def _rope128(x, cos, sin, lo):
    rot = jnp.where(lo, pltpu.roll(x, 96, 1), pltpu.roll(x, 32, 1))
    return x * cos + rot * sin


def _q_kernel(mq_ref, anw_ref, w_ref, qnw_ref, cos_ref, sin_ref, q_ref):
    xn = _rms(mq_ref[...], anw_ref[...]).astype(BF16)
    q = _dot(xn, w_ref[...])
    lo = lax.broadcasted_iota(jnp.int32, (1, LANES), 1) < 32
    cos = cos_ref[...]
    sin = sin_ref[...]
    wn = qnw_ref[:, 0:128] * Q_SCALE
    wr = qnw_ref[:, 128:256] * Q_SCALE
    for h in range(MLA_H):
        qn = q[:, h * HEAD_PAD:h * HEAD_PAD + 128]
        qr = q[:, h * HEAD_PAD + 128:(h + 1) * HEAD_PAD]
        ssq = jnp.sum(qn * qn + qr * qr, axis=-1, keepdims=True)
        rinv = lax.rsqrt(ssq / MLA_DQK + EPS)
        qn = qn * rinv * wn
        qr = _rope128(qr * rinv * wr, cos, sin, lo)
        q_ref[:, h * HEAD_PAD:h * HEAD_PAD + 128] = qn.astype(BF16)
        q_ref[:, h * HEAD_PAD + 128:(h + 1) * HEAD_PAD] = qr.astype(BF16)


def _qpath(p1, anw, w_uq, qnw, cos, sin):
    M = p1.shape[0]
    tr = cos.shape[0]
    tm = _pick(math.gcd(M, tr), 512)
    nper = tr // tm
    return pl.pallas_call(
        _q_kernel,
        grid=(M // tm,),
        in_specs=[pl.BlockSpec((tm, Q_LORA), lambda i: (i, 4)),
                  pl.BlockSpec((1, Q_LORA), lambda i: (0, 0)),
                  pl.BlockSpec((Q_LORA, MLA_H * HEAD_PAD), lambda i: (0, 0)),
                  pl.BlockSpec((1, HEAD_PAD), lambda i: (0, 0)),
                  pl.BlockSpec((tm, LANES), lambda i: (i % nper, 0)),
                  pl.BlockSpec((tm, LANES), lambda i: (i % nper, 0))],
        out_specs=pl.BlockSpec((tm, MLA_H * HEAD_PAD), lambda i: (i, 0)),
        out_shape=jax.ShapeDtypeStruct((M, MLA_H * HEAD_PAD), BF16),
        compiler_params=_cparams(("parallel",)),
        name="qpath",
    )(p1, anw, w_uq, qnw, cos, sin)


def _kv_kernel(mkv_ref, tail_ref, anw_ref, wuk_ref, wuv_ref, kw_ref, cos_ref, sin_ref, c_ref, *kv_refs):
    c = _rms(mkv_ref[...], anw_ref[...])
    c_ref[...] = c
    if not kv_refs:
        return
    k_ref, v_ref = kv_refs
    cb = c.astype(BF16)
    kn = _dot(cb, wuk_ref[...])
    v_ref[...] = _dot(cb, wuv_ref[...]).astype(BF16)
    lane = lax.broadcasted_iota(jnp.int32, (1, LANES), 1)
    pe = jnp.where(lane < MLA_ROPE, tail_ref[...], 0.0)
    ssq_pe = jnp.sum(pe * pe, axis=-1, keepdims=True)
    kr = _rope128(pe * kw_ref[:, 128:256], cos_ref[...], sin_ref[...], lane < 32)
    wn = kw_ref[:, 0:128]
    for h in range(MLA_H):
        knh = kn[:, h * 128:(h + 1) * 128]
        rinv = lax.rsqrt((jnp.sum(knh * knh, axis=-1, keepdims=True) + ssq_pe) / MLA_DQK + EPS)
        k_ref[:, h * HEAD_PAD:h * HEAD_PAD + 128] = (knh * rinv * wn).astype(BF16)
        k_ref[:, h * HEAD_PAD + 128:(h + 1) * HEAD_PAD] = (kr * rinv).astype(BF16)


def _kvprep(p1, p2, anw, wuk, wuv, kw, cos, sin, *, with_kv):
    M = p1.shape[0]
    tr = cos.shape[0]
    tm = _pick(math.gcd(M, tr), 512)
    nper = tr // tm
    out_specs = [pl.BlockSpec((tm, KV_LORA), lambda i: (i, 0))]
    out_shape = [jax.ShapeDtypeStruct((M, KV_LORA), F32)]
    if with_kv:
        out_specs += [pl.BlockSpec((tm, MLA_H * HEAD_PAD), lambda i: (i, 0)),
                      pl.BlockSpec((tm, MLA_H * MLA_DV), lambda i: (i, 0))]
        out_shape += [jax.ShapeDtypeStruct((M, MLA_H * HEAD_PAD), BF16),
                      jax.ShapeDtypeStruct((M, MLA_H * MLA_DV), BF16)]
    return pl.pallas_call(
        _kv_kernel,
        grid=(M // tm,),
        in_specs=[pl.BlockSpec((tm, KV_LORA), lambda i: (i, 5)),
                  pl.BlockSpec((tm, TAIL_COLS), lambda i: (i, 0)),
                  pl.BlockSpec((1, KV_LORA), lambda i: (0, 0)),
                  pl.BlockSpec((KV_LORA, MLA_H * MLA_NOPE), lambda i: (0, 0)),
                  pl.BlockSpec((KV_LORA, MLA_H * MLA_DV), lambda i: (0, 0)),
                  pl.BlockSpec((1, HEAD_PAD), lambda i: (0, 0)),
                  pl.BlockSpec((tm, LANES), lambda i: (i % nper, 0)),
                  pl.BlockSpec((tm, LANES), lambda i: (i % nper, 0))],
        out_specs=out_specs,
        out_shape=out_shape,
        compiler_params=_cparams(("parallel",)),
        name="kvprep",
    )(p1, p2, anw, wuk, wuv, kw, cos, sin)


def _attn_kernel(q_ref, k_ref, v_ref, o_ref, s_scr, m_scr, l_scr, acc_scr, *, TQ, TK):
    R = TQ // TK
    HL = TK // LANES
    nfull = pl.program_id(2) * R

    def fold(x, op):
        y = x[:, 0:LANES]
        for g in range(1, HL):
            y = op(y, x[:, g * LANES:(g + 1) * LANES])
        return y

    def kv_rows(j):
        return pl.ds(pl.multiple_of(j * TK, TK), TK)

    m_scr[...] = jnp.full((TQ, LANES), -jnp.inf, F32)

    U = 2 if R % 2 == 0 else 1

    def full_a(jj, carry):
        m = m_scr[...]
        for u in range(U):
            j = jj * U + u
            s = _dot_nt(q_ref[...], k_ref[kv_rows(j), :])
            s_scr[j] = s
            m = jnp.maximum(m, fold(s, jnp.maximum))
        m_scr[...] = m
        return carry

    lax.fori_loop(0, nfull // U, full_a, 0)
    tri = lax.broadcasted_iota(jnp.int32, (TK, TK), 1) <= lax.broadcasted_iota(jnp.int32, (TK, TK), 0)
    for d in range(R):
        r0 = d * TK
        j = nfull + d
        s = _dot_nt(q_ref[r0:, :], k_ref[kv_rows(j), :])
        sd = jnp.where(tri, s[0:TK], -jnp.inf)
        s_scr[j, r0:r0 + TK, :] = sd
        m_scr[r0:r0 + TK, :] = jnp.maximum(m_scr[r0:r0 + TK, :], fold(sd, jnp.maximum))
        if r0 + TK < TQ:
            s_scr[j, r0 + TK:, :] = s[TK:]
            m_scr[r0 + TK:, :] = jnp.maximum(m_scr[r0 + TK:, :], fold(s[TK:], jnp.maximum))

    m_scr[...] = jnp.broadcast_to(jnp.max(m_scr[...], axis=-1, keepdims=True), (TQ, LANES))
    l_scr[...] = jnp.zeros((TQ, LANES), F32)
    acc_scr[...] = jnp.zeros((TQ, MLA_DV), F32)

    def probs(s, m):
        return jnp.concatenate([jnp.exp2(s[:, g * LANES:(g + 1) * LANES] - m) for g in range(HL)], axis=1)

    def full_b(jj, carry):
        ps = [probs(s_scr[jj * U + u], m_scr[...]) for u in range(U)]
        l_scr[...] += functools.reduce(jnp.add, [fold(p, jnp.add) for p in ps])
        pv = jnp.concatenate(ps, axis=1).astype(BF16)
        acc_scr[...] += _dot(pv, v_ref[pl.ds(pl.multiple_of(jj * (U * TK), U * TK), U * TK), :])
        return carry

    lax.fori_loop(0, nfull // U, full_b, 0)
    for d in range(R):
        r0 = d * TK
        j = nfull + d
        p = probs(s_scr[j, r0:, :], m_scr[r0:, :])
        l_scr[r0:, :] += fold(p, jnp.add)
        acc_scr[r0:, :] += _dot(p.astype(BF16), v_ref[kv_rows(j), :])
    o_ref[...] = (acc_scr[...] / jnp.sum(l_scr[...], axis=-1, keepdims=True)).astype(o_ref.dtype)


def _attn(q, k, v, *, B, L):
    TK = 256
    TQ = _pick(L, 1024, TK)
    NQ = L // TQ
    kern = functools.partial(_attn_kernel, TQ=TQ, TK=TK)
    return pl.pallas_call(
        kern,
        grid=(B, MLA_H, NQ),
        in_specs=[pl.BlockSpec((TQ, HEAD_PAD), lambda b, h, i: (b * NQ + i, h)),
                  pl.BlockSpec((L, HEAD_PAD), lambda b, h, i: (b, h)),
                  pl.BlockSpec((L, MLA_DV), lambda b, h, i: (b, h))],
        out_specs=pl.BlockSpec((TQ, MLA_DV), lambda b, h, i: (b * NQ + i, h)),
        out_shape=jax.ShapeDtypeStruct((B * L, MLA_H * MLA_DV), BF16),
        scratch_shapes=[pltpu.VMEM((L // TK, TQ, TK), F32), pltpu.VMEM((TQ, LANES), F32),
                        pltpu.VMEM((TQ, LANES), F32), pltpu.VMEM((TQ, MLA_DV), F32)],
        compiler_params=_cparams(("parallel", "parallel", "arbitrary")),
        name="attn",
    )(q, k, v)


def _qs_kernel(q_ref, wukT_ref, kw_ref, qlat_ref, a_ref, *, DB, DL):
    q = q_ref[...].astype(F32)
    qn = (q[:, 0:128] * kw_ref[:, 0:128]).astype(BF16)
    qlat_ref[...] = _dot(qn, wukT_ref[...]).reshape(DB, 1, DL, KV_LORA)
    qr = q[:, 128:256]
    wr = kw_ref[:, 128:256]
    lo = lax.broadcasted_iota(jnp.int32, (1, LANES), 1) < 32
    a1 = qr * wr
    a2 = jnp.where(lo, pltpu.roll(qr, 96, 1), -pltpu.roll(qr, 32, 1)) * wr
    a_ref[...] = (a1 + pltpu.roll(a2, 64, 1)).reshape(DB, 1, DL, LANES)


def _qs(q_s, wukT, kw, *, DB, DL):
    kern = functools.partial(_qs_kernel, DB=DB, DL=DL)
    return pl.pallas_call(
        kern,
        grid=(MLA_H,),
        in_specs=[pl.BlockSpec((DB * DL, HEAD_PAD), lambda h: (0, h)),
                  pl.BlockSpec((MLA_NOPE, KV_LORA), lambda h: (h, 0)),
                  pl.BlockSpec((1, HEAD_PAD), lambda h: (0, 0))],
        out_specs=[pl.BlockSpec((DB, 1, DL, KV_LORA), lambda h: (0, h, 0, 0)),
                   pl.BlockSpec((DB, 1, DL, LANES), lambda h: (0, h, 0, 0))],
        out_shape=[jax.ShapeDtypeStruct((DB, MLA_H, DL, KV_LORA), F32),
                   jax.ShapeDtypeStruct((DB, MLA_H, DL, LANES), F32)],
        compiler_params=_cparams(("parallel",)),
        name="qs",
    )(q_s, wukT, kw)


def _decode_kernel(pt_ref, lat_hbm, pet_hbm, cst_ref, csn_ref, qlat_ref, a_ref, wukT_ref, cnew_ref, tnew_ref,
                   wuv_ref, y_ref, lat_buf, pet_buf, sem, wall_scr, cb_scr, k2t_scr, m_scr, l_scr, acc_scr,
                   *, PPS, NS, DL, CP, DB, layer):
    b = pl.program_id(0)
    s = pl.program_id(1)
    NR = MLA_H * DL
    CK = CP * PAGE

    def page_copies(bb, ss, slot):
        base = (bb * NS + ss) * PPS
        cps = []
        for i in range(PPS):
            pg = pt_ref[base + i]
            keys = pl.ds(i * PAGE, PAGE)
            cps.append(pltpu.make_async_copy(lat_hbm.at[layer, pg], lat_buf.at[slot, keys, :], sem.at[0, slot]))
            cps.append(pltpu.make_async_copy(pet_hbm.at[layer, pg], pet_buf.at[slot, :, keys], sem.at[1, slot]))
        return cps

    @pl.when((b == 0) & (s == 0))
    def _():
        wall_scr[0:MLA_H * MLA_NOPE, :] = wukT_ref[...]
        for cp in page_copies(0, 0, 0):
            cp.start()

    @pl.when(s == 0)
    def _():
        wall_scr[MLA_H * MLA_NOPE:, :] = qlat_ref[...].reshape(NR, KV_LORA).astype(BF16)
        m_scr[...] = jnp.full((NR, 1), -jnp.inf, F32)
        l_scr[...] = jnp.zeros((NR, 1), F32)
        acc_scr[...] = jnp.zeros((NR, KV_LORA), F32)

    a = a_ref[...].reshape(NR, LANES).astype(BF16)

    def scores(cb, s_rope, ssq_pe):
        n = cb.shape[0]
        big = _dot_nt(wall_scr[...], cb)
        kn = big[0:MLA_H * MLA_NOPE, :].reshape(MLA_H, MLA_NOPE, n)
        rinv = lax.rsqrt((jnp.sum(kn * kn, axis=1) + ssq_pe) / MLA_DQK + EPS)
        sc = big[MLA_H * MLA_NOPE:, :] + s_rope
        return (sc.reshape(MLA_H, DL, n) * rinv[:, None, :]).reshape(NR, n)

    def update(sc, vals):
        m_old = m_scr[...]
        m_new = jnp.maximum(m_old, jnp.max(sc, axis=-1, keepdims=True))
        alpha = jnp.exp2(m_old - m_new)
        p = jnp.exp2(sc - m_new)
        l_scr[...] = l_scr[...] * alpha + jnp.sum(p, axis=-1, keepdims=True)
        acc_scr[...] = acc_scr[...] * alpha + _dot(p.astype(BF16), vals)
        m_scr[...] = m_new

    @pl.when(s < NS)
    def _():
        slot = (b * NS + s) % 2
        for cp in page_copies(b, s, slot):
            cp.wait()
        last = s == NS - 1
        nb = jnp.where(last, b + 1, b)
        ns = jnp.where(last, 0, s + 1)

        @pl.when(nb < DB)
        def _():
            for cp in page_copies(nb, ns, 1 - slot):
                cp.start()

        chunks = []
        for c in range(PPS // CP):
            sq = []
            for i in range(c * CP, (c + 1) * CP):
                keys = slice(i * PAGE, (i + 1) * PAGE)
                cb_scr[keys, :] = lat_buf[slot, keys, :].astype(BF16)
                pet = pet_buf[slot, :, keys]
                k2t_scr[0:MLA_ROPE, keys] = (pet * cst_ref[0:MLA_ROPE, keys]).astype(BF16)
                k2t_scr[MLA_ROPE:, keys] = (pet * cst_ref[MLA_ROPE:, keys]).astype(BF16)
                sq.append(jnp.sum(pet * pet, axis=0, keepdims=True))
            ck = slice(c * CK, (c + 1) * CK)
            s_rope = _dot(a, k2t_scr[:, ck])
            chunks.append((scores(cb_scr[ck, :], s_rope, jnp.concatenate(sq, axis=1)), ck))
        for sc, ck in chunks:
            update(sc, cb_scr[ck, :])

    @pl.when(s == NS)
    def _():
        lane = lax.broadcasted_iota(jnp.int32, (1, LANES), 1)
        pad = lambda x: jnp.concatenate([x, jnp.zeros((PAGE - DL, x.shape[1]), F32)], axis=0)
        cn = pad(cnew_ref[...]).astype(BF16)
        pe = jnp.where(lane < MLA_ROPE, tnew_ref[...], 0.0)
        k2 = pad((pe + pltpu.roll(pe, 64, 1)) * csn_ref[...]).astype(BF16)
        sq = pad(pe * pe)
        sqh = sq.astype(BF16)
        sql = (sq - sqh.astype(F32)).astype(BF16)
        ones = jnp.ones((MLA_H, LANES), BF16)
        sc = scores(cn, _dot_nt(a, k2), _dot_nt(ones, sqh) + _dot_nt(ones, sql))
        qidx = lax.broadcasted_iota(jnp.int32, (MLA_H, DL, PAGE), 1).reshape(NR, PAGE)
        kidx = lax.broadcasted_iota(jnp.int32, (NR, PAGE), 1)
        update(jnp.where(kidx <= qidx, sc, -jnp.inf), cn)
        o = (acc_scr[...] / l_scr[...]).astype(BF16)
        full = _dot(o, wuv_ref[...])
        for h in range(MLA_H):
            y_ref[:, h * MLA_DV:(h + 1) * MLA_DV] = full[h * DL:(h + 1) * DL, h * MLA_DV:(h + 1) * MLA_DV]


def _decode(page_table, cache_lat, cache_pet, layer, cst, csn, qlat, a, wukT, c_new, tail_new, wuv, *, DB, DL, PPS):
    n_pages = page_table.shape[1]
    NS = n_pages // PPS
    TK = PPS * PAGE
    NR = MLA_H * DL
    CP = math.gcd(PPS, 4)

    in_specs = [pl.BlockSpec(memory_space=pl.ANY),
                pl.BlockSpec(memory_space=pl.ANY),
                pl.BlockSpec((LANES, TK), lambda b, s, pt: (0, jnp.minimum(s, NS - 1))),
                pl.BlockSpec((DL, LANES), lambda b, s, pt: (0, 0)),
                pl.BlockSpec((None, MLA_H, DL, KV_LORA), lambda b, s, pt: (b, 0, 0, 0)),
                pl.BlockSpec((None, MLA_H, DL, LANES), lambda b, s, pt: (b, 0, 0, 0)),
                pl.BlockSpec((MLA_H * MLA_NOPE, KV_LORA), lambda b, s, pt: (0, 0)),
                pl.BlockSpec((DL, KV_LORA), lambda b, s, pt: (b, 0)),
                pl.BlockSpec((DL, TAIL_COLS), lambda b, s, pt: (b, 0)),
                pl.BlockSpec((KV_LORA, MLA_H * MLA_DV), lambda b, s, pt: (0, 0))]
    grid_spec = pltpu.PrefetchScalarGridSpec(
        num_scalar_prefetch=1,
        grid=(DB, NS + 1),
        in_specs=in_specs,
        out_specs=pl.BlockSpec((DL, MLA_H * MLA_DV), lambda b, s, pt: (b, 0)),
        scratch_shapes=[pltpu.VMEM((2, TK, KV_LORA), F32),
                        pltpu.VMEM((2, MLA_ROPE, TK), F32),
                        pltpu.SemaphoreType.DMA((2, 2)),
                        pltpu.VMEM((MLA_H * MLA_NOPE + NR, KV_LORA), BF16),
                        pltpu.VMEM((TK, KV_LORA), BF16),
                        pltpu.VMEM((LANES, TK), BF16),
                        pltpu.VMEM((NR, 1), F32),
                        pltpu.VMEM((NR, 1), F32),
                        pltpu.VMEM((NR, KV_LORA), F32)])
    kern = functools.partial(_decode_kernel, PPS=PPS, NS=NS, DL=DL, CP=CP, DB=DB, layer=layer)
    return pl.pallas_call(
        kern,
        grid_spec=grid_spec,
        out_shape=jax.ShapeDtypeStruct((DB * DL, MLA_H * MLA_DV), F32),
        compiler_params=_cparams(("arbitrary", "arbitrary")),
        name="decode",
    )(page_table.reshape(-1), cache_lat, cache_pet, cst, csn, qlat, a, wukT, c_new, tail_new, wuv)


def _merge_kernel(g0_ref, g1_ref, g2_ref, bg_ref, yp_ref, yg_ref, ym_ref, x_ref, w_ref, o_ref):
    def gate(g_ref, i):
        return 1.0 / (1.0 + jnp.exp(-(g_ref[...].astype(F32) + bg_ref[i:i + 1, :])))

    merged = gate(g0_ref, 0) * yp_ref[...] + gate(g1_ref, 1) * yg_ref[...] + gate(g2_ref, 2) * ym_ref[...]
    o_ref[...] = x_ref[...] + _dot(merged.astype(BF16), w_ref[...])


def _merge(p1, bg, yp, yg, ym, x, w_out):
    M = x.shape[0]
    tm = _pick(M, 512)
    row = lambda i: (i, 0)
    return pl.pallas_call(
        _merge_kernel,
        grid=(M // tm,),
        in_specs=[pl.BlockSpec((tm, D), lambda i: (i, 3)),
                  pl.BlockSpec((tm, D), lambda i: (i, 4)),
                  pl.BlockSpec((tm, D), lambda i: (i, 5)),
                  pl.BlockSpec((3, D), lambda i: (0, 0)),
                  pl.BlockSpec((tm, D), row), pl.BlockSpec((tm, D), row), pl.BlockSpec((tm, D), row),
                  pl.BlockSpec((tm, D), row),
                  pl.BlockSpec((D, D), lambda i: (0, 0))],
        out_specs=pl.BlockSpec((tm, D), row),
        out_shape=jax.ShapeDtypeStruct((M, D), F32),
        compiler_params=_cparams(("parallel",)),
        name="merge",
    )(p1, p1, p1, bg, yp, yg, ym, x, w_out)


def _mlp_kernel(x_ref, nw_ref, wu_ref, wd_ref, o_ref, h_scr, acc_scr):
    f = pl.program_id(1)

    @pl.when(f == 0)
    def _():
        h_scr[...] = _rms(x_ref[...], nw_ref[...]).astype(BF16)
        acc_scr[...] = jnp.zeros_like(acc_scr)

    a = jnp.maximum(_dot(h_scr[...], wu_ref[...]), 0.0)
    acc_scr[...] += _dot((a * a).astype(BF16), wd_ref[...])

    @pl.when(f == pl.num_programs(1) - 1)
    def _():
        o_ref[...] = x_ref[...] + acc_scr[...]


def _mlp(x, nw, w_up, w_down):
    M = x.shape[0]
    tm = _pick(M, 1024)
    tf = 512
    return pl.pallas_call(
        _mlp_kernel,
        grid=(M // tm, D_FF // tf),
        in_specs=[pl.BlockSpec((tm, D), lambda i, f: (i, 0)),
                  pl.BlockSpec((1, D), lambda i, f: (0, 0)),
                  pl.BlockSpec((D, tf), lambda i, f: (0, f)),
                  pl.BlockSpec((tf, D), lambda i, f: (f, 0))],
        out_specs=pl.BlockSpec((tm, D), lambda i, f: (i, 0)),
        out_shape=jax.ShapeDtypeStruct((M, D), F32),
        scratch_shapes=[pltpu.VMEM((tm, D), BF16), pltpu.VMEM((tm, D), F32)],
        compiler_params=_cparams(("parallel", "arbitrary")),
        name="mlp",
    )(x, nw, w_up, w_down)


def _rope_angles(pos):
    inv = 1.0 / (ROPE_THETA ** (jnp.arange(0, MLA_ROPE, 2, dtype=F32) / MLA_ROPE))
    return pos.astype(F32)[:, None] * inv[None, :]


def _token_tables(pos):
    ang = _rope_angles(pos)
    c, s = jnp.cos(ang), jnp.sin(ang)
    z = jnp.zeros((pos.shape[0], LANES - MLA_ROPE), F32)
    return jnp.concatenate([c, c, z], axis=1), jnp.concatenate([-s, s, z], axis=1)


def _key_table(pos):
    ang = _rope_angles(pos)
    c, s = jnp.cos(ang), jnp.sin(ang)
    return jnp.concatenate([c, c, s, s], axis=1)


def _pad_heads(w):
    lead = w.shape[:-1]
    w = w.reshape(lead + (MLA_H, MLA_DQK))
    w = jnp.concatenate([w, jnp.zeros(lead + (MLA_H, HEAD_PAD - MLA_DQK), w.dtype)], axis=-1)
    return w.reshape(lead + (MLA_H * HEAD_PAD,))


def kernel(x_prompt, x_sample, state_pool, state_gla, cache_kv_latent, cache_k_rope, page_table,
           norm1_w, w_in, b_gate, w_pool, pool_scale, w_gla_a2, b_gla_a, gla_norm_w,
           q_a_norm_w, kv_a_norm_w, w_uq, q_norm_w, w_uk, w_uv, k_norm_w, w_out,
           norm2_w, w_up, w_down):
    return _forward(x_prompt, x_sample, state_pool, state_gla, cache_kv_latent, cache_k_rope, page_table,
                    norm1_w, w_in, b_gate, w_pool, pool_scale, w_gla_a2, b_gla_a, gla_norm_w,
                    q_a_norm_w, kv_a_norm_w, w_uq, q_norm_w, w_uk, w_uv, k_norm_w, w_out,
                    norm2_w, w_up, w_down, pages_per_step=16)


def _forward(x_prompt, x_sample, state_pool, state_gla, cache_kv_latent, cache_k_rope, page_table,
             norm1_w, w_in, b_gate, w_pool, pool_scale, w_gla_a2, b_gla_a, gla_norm_w,
             q_a_norm_w, kv_a_norm_w, w_uq, q_norm_w, w_uk, w_uv, k_norm_w, w_out,
             norm2_w, w_up, w_down, *, pages_per_step):
    B, L, _ = x_prompt.shape
    DB, DL, _ = x_sample.shape
    depth = w_in.shape[0]
    n_pages = page_table.shape[1]
    past = n_pages * PAGE
    PPS = pages_per_step
    assert n_pages % PPS == 0

    o = [0]
    for n in (1024, 512, 512, 1024, GLA_RANK, 1024, Q_LORA, KV_LORA, MLA_ROPE, 3 * D):
        o.append(o[-1] + n)
    seg = lambda i: w_in[:, :, o[i]:o[i + 1]]
    w_main = jnp.concatenate([seg(0), seg(6), seg(7), seg(1), seg(2), seg(3), seg(5), seg(9)], axis=-1).astype(BF16)
    w_tail = jnp.concatenate([seg(8), seg(4), jnp.zeros((depth, D, TAIL_COLS - MLA_ROPE - GLA_RANK), F32)],
                             axis=-1).astype(BF16)
    wa_pad = jnp.zeros((depth, TAIL_COLS, GLA_H * GLA_DK), F32).at[:, MLA_ROPE:MLA_ROPE + GLA_RANK, :].set(w_gla_a2)
    wa_pad = wa_pad.astype(BF16)
    w_pool_b = w_pool.astype(BF16)
    w_uq_p = _pad_heads(w_uq).astype(BF16)
    qnw_p = jnp.concatenate([q_norm_w, jnp.zeros((depth, HEAD_PAD - MLA_DQK), F32)], axis=-1)[:, None, :]
    knw_p = jnp.concatenate([k_norm_w, jnp.zeros((depth, HEAD_PAD - MLA_DQK), F32)], axis=-1)[:, None, :]
    w_uk_b = w_uk.astype(BF16)
    w_ukT_b = jnp.swapaxes(w_uk, 1, 2).astype(BF16)
    w_uv_b = w_uv.astype(BF16)
    w_out_b = w_out.astype(BF16)
    w_up_b = w_up.astype(BF16)
    w_down_b = w_down.astype(BF16)

    cos_p, sin_p = _token_tables(jnp.arange(L))
    reps = math.gcd(DB, 512 // DL)
    cos_s, sin_s = _token_tables(jnp.tile(past + jnp.arange(DL), reps))
    cs_past_t = _key_table(jnp.arange(past)).T
    cs_new = _key_table(past + jnp.arange(DL))
    cache_pet = jnp.swapaxes(cache_k_rope, 2, 3)

    xp = x_prompt.reshape(B * L, D)
    xs = x_sample.reshape(DB * DL, D)
    zero_prefix = jnp.zeros((B, POOL_STATE, D), F32)
    zero_state = jnp.zeros((B, GLA_H, GLA_DK, GLA_DV), F32)

    outs = [[] for _ in range(8)]
    for l in range(depth):
        r1 = lambda w: w[l][None, :]
        p1, pb, p2 = _proj(xp, r1(norm1_w), w_main[l], w_tail[l])
        yp, pool_p = _pool(p1, zero_prefix, w_pool_b[l], r1(pool_scale), B=B, L=L, start_pos=0)
        yg, gla_p = _gla(pb, p2, zero_state, wa_pad[l], r1(b_gla_a), r1(gla_norm_w), B=B, L=L)
        q = _qpath(p1, r1(q_a_norm_w), w_uq_p[l], qnw_p[l], cos_p, sin_p)
        c_p, k, v = _kvprep(p1, p2, r1(kv_a_norm_w), w_uk_b[l], w_uv_b[l], knw_p[l], cos_p, sin_p, with_kv=True)
        ym = _attn(q, k, v, B=B, L=L)
        xp = _merge(pb, b_gate[l], yp, yg, ym, xp, w_out_b[l])
        xp = _mlp(xp, r1(norm2_w), w_up_b[l], w_down_b[l])
        pe_p = p2[:, :MLA_ROPE]
        s1, sb, s2 = _proj(xs, r1(norm1_w), w_main[l], w_tail[l])
        yp, pool_s = _pool(s1, state_pool[l], w_pool_b[l], r1(pool_scale), B=DB, L=DL, start_pos=past)
        yg, gla_s = _gla(sb, s2, state_gla[l], wa_pad[l], r1(b_gla_a), r1(gla_norm_w), B=DB, L=DL)
        q = _qpath(s1, r1(q_a_norm_w), w_uq_p[l], qnw_p[l], cos_s, sin_s)
        (c_s,) = _kvprep(s1, s2, r1(kv_a_norm_w), w_uk_b[l], w_uv_b[l], knw_p[l], cos_s, sin_s, with_kv=False)
        qlat, a = _qs(q, w_ukT_b[l], knw_p[l], DB=DB, DL=DL)
        ym = _decode(page_table, cache_kv_latent, cache_pet, l, cs_past_t, cs_new, qlat, a, w_ukT_b[l], c_s, s2,
                     w_uv_b[l], DB=DB, DL=DL, PPS=PPS)
        xs = _merge(sb, b_gate[l], yp, yg, ym, xs, w_out_b[l])
        xs = _mlp(xs, r1(norm2_w), w_up_b[l], w_down_b[l])
        pe_s = s2[:, :MLA_ROPE]
        for lst, val in zip(outs, (pool_p, pool_s, gla_p, gla_s,
                                   c_p.reshape(B, L, KV_LORA), c_s.reshape(DB, DL, KV_LORA),
                                   pe_p.reshape(B, L, MLA_ROPE), pe_s.reshape(DB, DL, MLA_ROPE))):
            lst.append(val)
    return (xp.reshape(B, L, D), xs.reshape(DB, DL, D)) + tuple(jnp.stack(v) for v in outs)
```

```python
import functools
import math

import jax
import jax.numpy as jnp
from jax import lax
from jax.experimental import pallas as pl
from jax.experimental.pallas import tpu as pltpu

F32 = jnp.float32
BF16 = jnp.bfloat16
EPS = 1e-6

D = 1024
POOL_WINDOWS = (2, 4, 8, 16)
POOL_GW = 256
POOL_STATE = 15
POOL_HDR = 32
GLA_H = 4
GLA_DK = 128
GLA_DV = 256
GLA_RANK = 16
GLA_TAU = 16.0
GLA_CHUNK = 64
MLA_H = 8
MLA_NOPE = 128
MLA_ROPE = 64
MLA_DQK = MLA_NOPE + MLA_ROPE
MLA_DV = 128
Q_LORA = 256
KV_LORA = 256
ROPE_THETA = 10000.0
ATTN_SCALE = MLA_DQK ** -0.5
Q_SCALE = ATTN_SCALE * math.log2(math.e)
D_FF = 4 * D
PAGE = 128
HEAD_PAD = 256
LANES = 128

F32_COLS = 1536
B16_COLS = 6144
TAIL_COLS = 128

NT_DIMS = (((1,), (1,)), ((), ()))
TN_DIMS = (((0,), (0,)), ((), ()))


def _cparams(sem, vmem_mb=48):
    return pltpu.CompilerParams(dimension_semantics=sem, vmem_limit_bytes=vmem_mb * 1024 * 1024)


def _dot(a, b):
    return jnp.dot(a, b, preferred_element_type=F32)


def _dot_nt(a, b):
    return lax.dot_general(a, b, NT_DIMS, preferred_element_type=F32)


def _dot_tn(a, b):
    return lax.dot_general(a, b, TN_DIMS, preferred_element_type=F32)


def _rms(x, w):
    return x * lax.rsqrt(jnp.mean(x * x, axis=-1, keepdims=True) + EPS) * w


def _pick(n, target, mult=16):
    best = None
    for t in range(mult, min(n, target) + 1, mult):
        if n % t == 0:
            best = t
    assert best is not None, (n, target)
    return best


def _proj_kernel(x_ref, nw_ref, w_ref, wt_ref, of_ref, ob_ref, ot_ref, h_scr, *, NF):
    j = pl.program_id(1)

    @pl.when(j == 0)
    def _():
        h = _rms(x_ref[...], nw_ref[...]).astype(BF16)
        h_scr[...] = h
        ot_ref[...] = _dot(h, wt_ref[...])

    r = _dot(h_scr[...], w_ref[...])

    @pl.when(j < NF)
    def _():
        of_ref[...] = r

    @pl.when(j >= NF)
    def _():
        ob_ref[...] = r.astype(BF16)


def _proj(x, nw, w_main, w_tail):
    M = x.shape[0]
    tm = _pick(M, 1024)
    tn = 512
    NF = F32_COLS // tn
    return pl.pallas_call(
        functools.partial(_proj_kernel, NF=NF),
        grid=(M // tm, (F32_COLS + B16_COLS) // tn),
        in_specs=[pl.BlockSpec((tm, D), lambda i, j: (i, 0)),
                  pl.BlockSpec((1, D), lambda i, j: (0, 0)),
                  pl.BlockSpec((D, tn), lambda i, j: (0, j)),
                  pl.BlockSpec((D, TAIL_COLS), lambda i, j: (0, 0))],
        out_specs=[pl.BlockSpec((tm, tn), lambda i, j: (i, jnp.minimum(j, NF - 1))),
                   pl.BlockSpec((tm, tn), lambda i, j: (i, jnp.maximum(j - NF, 0))),
                   pl.BlockSpec((tm, TAIL_COLS), lambda i, j: (i, 0))],
        out_shape=[jax.ShapeDtypeStruct((M, F32_COLS), F32),
                   jax.ShapeDtypeStruct((M, B16_COLS), BF16),
                   jax.ShapeDtypeStruct((M, TAIL_COLS), F32)],
        scratch_shapes=[pltpu.VMEM((tm, D), BF16)],
        compiler_params=_cparams(("parallel", "arbitrary")),
        name="proj",
    )(x, nw, w_main, w_tail)


def _pool_kernel(u_ref, pre_ref, w_ref, sc_ref, y_ref, st_ref, ext_scr, lvl_scr, pooled_scr,
                 *, SB, T, NT, start_pos):
    t = pl.program_id(1)
    H, R0, GW = POOL_HDR, 16, POOL_GW
    pos = start_pos + t * T + lax.broadcasted_iota(jnp.int32, (T, 1), 0)

    @pl.when(t == 0)
    def _():
        lvl_scr[:, 0:R0, :] = jnp.zeros((3, R0, D), F32)

    for s in range(SB):
        @pl.when(t == 0)
        def _():
            ext_scr[s, 0:H - POOL_STATE, :] = jnp.zeros((H - POOL_STATE, D), F32)
            ext_scr[s, H - POOL_STATE:H, :] = pre_ref[s]

        ext_scr[s, H:H + T, :] = u_ref[s * T:(s + 1) * T, :]
        a1 = ext_scr[s, R0:H + T, :] + ext_scr[s, R0 - 1:H + T - 1, :]
        lvl_scr[0, R0:H + T, :] = a1
        a2 = a1[:, GW:] + lvl_scr[0, R0 - 2:H + T - 2, GW:]
        lvl_scr[1, R0:H + T, GW:] = a2
        a3 = a2[:, GW:] + lvl_scr[1, R0 - 4:H + T - 4, 2 * GW:]
        lvl_scr[2, R0:H + T, 2 * GW:] = a3
        a4 = a3[H - R0:, GW:] + lvl_scr[2, H - 8:H + T - 8, 3 * GW:]
        wsum = (a1[H - R0:, 0:GW], a2[H - R0:, 0:GW], a3[H - R0:, 0:GW], a4)
        for g, w in enumerate(POOL_WINDOWS):
            cols = slice(g * GW, (g + 1) * GW)
            cnt = jnp.minimum(pos + 1, w).astype(F32)
            pooled_scr[s * T:(s + 1) * T, cols] = (wsum[g] / cnt - ext_scr[s, H:H + T, cols]).astype(BF16)
        last = ext_scr[s, H + T - POOL_STATE:H + T, :]

        @pl.when(t == NT - 1)
        def _():
            st_ref[s] = last

        ext_scr[s, H - POOL_STATE:H, :] = last
    for g in range(len(POOL_WINDOWS)):
        cols = slice(g * POOL_GW, (g + 1) * POOL_GW)
        y_ref[:, cols] = (_dot(pooled_scr[:, cols], w_ref[g]) * sc_ref[:, cols]).astype(y_ref.dtype)


def _pool(p1, prefix, w_pool, scale, *, B, L, start_pos):
    if L >= 512:
        SB, T = 1, 512
    else:
        SB, T = _pick(B, 16, 1), L
    NT = L // T
    kern = functools.partial(_pool_kernel, SB=SB, T=T, NT=NT, start_pos=start_pos)
    return pl.pallas_call(
        kern,
        grid=(B // SB, NT),
        in_specs=[pl.BlockSpec((SB * T, D), lambda g, t: (g * NT + t, 0)),
                  pl.BlockSpec((SB, POOL_STATE, D), lambda g, t: (g, 0, 0)),
                  pl.BlockSpec((4, POOL_GW, POOL_GW), lambda g, t: (0, 0, 0)),
                  pl.BlockSpec((1, D), lambda g, t: (0, 0))],
        out_specs=[pl.BlockSpec((SB * T, D), lambda g, t: (g * NT + t, 0)),
                   pl.BlockSpec((SB, POOL_STATE, D), lambda g, t: (g, 0, 0))],
        out_shape=[jax.ShapeDtypeStruct((B * L, D), BF16),
                   jax.ShapeDtypeStruct((B, POOL_STATE, D), F32)],
        scratch_shapes=[pltpu.VMEM((SB, POOL_HDR + T, D), F32), pltpu.VMEM((3, POOL_HDR + T, D), F32),
                        pltpu.VMEM((SB * T, D), BF16)],
        compiler_params=_cparams(("parallel", "arbitrary")),
        name="pool",
    )(p1, prefix, w_pool, scale)


def _gla_kernel(q_ref, k_ref, v_ref, r_ref, tail_ref, s0_ref, wa_ref, ba_ref, nw_ref,
                o_ref, sout_ref, st_scr, la_scr, *, SB, T, C, NT):
    t = pl.program_id(1)
    z = _dot(tail_ref[...].astype(BF16), wa_ref[...]) + ba_ref[...]
    la_scr[...] = (jnp.minimum(z, 0.0) - jnp.log(1.0 + jnp.exp(-jnp.abs(z)))) * (1.0 / GLA_TAU)

    @pl.when(t == 0)
    def _():
        for s in range(SB):
            for h in range(GLA_H):
                if not (T == C and NT == 1):
                    st_scr[s, h] = s0_ref[s, h].T

    row = lax.broadcasted_iota(jnp.int32, (C, C), 0)
    col = lax.broadcasted_iota(jnp.int32, (C, C), 1)
    tril = row >= col
    tri = tril.astype(F32).astype(BF16)
    nw = nw_ref[...]
    NC = T // C
    ONE = NC == 1 and NT == 1
    pre = [x[...].astype(F32) for x in (q_ref, k_ref, v_ref, r_ref)] if NC == 1 else None

    def load(i, ref, rows, cols=slice(None)):
        return pre[i][rows, cols] if pre is not None else ref[rows, cols].astype(F32)

    def chunk(s, c):
        if isinstance(c, int):
            rows = slice(s * T + c * C, s * T + (c + 1) * C)
        else:
            rows = pl.ds(pl.multiple_of(s * T + c * C, C), C)
        la = la_scr[rows, :]
        la_hi = la.astype(BF16)
        la_lo = (la - la_hi.astype(F32)).astype(BF16)
        b = _dot(tri, la_hi) + _dot(tri, la_lo)
        bl = b[C - 1:C, :]
        k = load(1, k_ref, rows)
        qt_all = (load(0, q_ref, rows) * (GLA_DK ** -0.5) * jnp.exp(b)).astype(BF16)
        kt_all = (k * jnp.exp(-b)).astype(BF16)
        kd_all = (k * jnp.exp(bl - b)).astype(BF16)
        dec = jnp.exp(bl)
        if ONE:
            rid = lax.broadcasted_iota(jnp.int32, (C, 1), 0)
            d_hi = dec.astype(BF16).astype(F32)
            dec2 = jnp.where(rid == 0, d_hi, jnp.where(rid == 1, dec - d_hi, 0.0)).astype(BF16)
            ones2 = jnp.ones((C, GLA_DV), BF16)
        for h in range(GLA_H):
            kc = slice(h * GLA_DK, (h + 1) * GLA_DK)
            vc = slice(h * GLA_DV, (h + 1) * GLA_DV)
            qt = qt_all[:, kc]
            vb = load(2, v_ref, rows, vc).astype(BF16)
            att = jnp.where(tril, _dot_nt(qt, kt_all[:, kc]), 0.0)
            if ONE:
                st = s0_ref[s, h]
                o = _dot(att.astype(BF16), vb) + _dot(qt, st.astype(BF16))
                sout_ref[s, h] = st * _dot_tn(dec2[:, kc], ones2) + _dot_tn(kd_all[:, kc], vb)
            else:
                st = st_scr[s, h]
                o = _dot(att.astype(BF16), vb) + _dot_nt(qt, st.astype(BF16))
                st_scr[s, h] = st * dec[:, kc] + _dot_tn(vb, kd_all[:, kc])
            r = load(3, r_ref, rows, vc)
            o_ref[rows, vc] = (_rms(o, nw) * (r * (1.0 / (1.0 + jnp.exp(-r))))).astype(o_ref.dtype)

    for s in range(SB):
        if NC == 1:
            chunk(s, 0)
        else:
            def body(c, carry, s=s):
                chunk(s, c)
                return carry
            lax.fori_loop(0, NC, body, 0, unroll=4)

    @pl.when(t == NT - 1)
    def _():
        for s in range(SB):
            for h in range(GLA_H):
                if not ONE:
                    sout_ref[s, h] = st_scr[s, h].T


def _gla(p1, p2, s0, wa, ba, nw, *, B, L):
    C = math.gcd(L, GLA_CHUNK)
    if L >= 512:
        SB, T = 1, 512
    else:
        SB, T = _pick(B, 8, 1), L
    NT = L // T
    R = SB * T
    kern = functools.partial(_gla_kernel, SB=SB, T=T, C=C, NT=NT)
    return pl.pallas_call(
        kern,
        grid=(B // SB, NT),
        in_specs=[pl.BlockSpec((R, 512), lambda g, t: (g * NT + t, 0)),
                  pl.BlockSpec((R, 512), lambda g, t: (g * NT + t, 1)),
                  pl.BlockSpec((R, D), lambda g, t: (g * NT + t, 1)),
                  pl.BlockSpec((R, D), lambda g, t: (g * NT + t, 2)),
                  pl.BlockSpec((R, TAIL_COLS), lambda g, t: (g * NT + t, 0)),
                  pl.BlockSpec((SB, GLA_H, GLA_DK, GLA_DV), lambda g, t: (g, 0, 0, 0)),
                  pl.BlockSpec((TAIL_COLS, GLA_H * GLA_DK), lambda g, t: (0, 0)),
                  pl.BlockSpec((1, GLA_H * GLA_DK), lambda g, t: (0, 0)),
                  pl.BlockSpec((1, GLA_DV), lambda g, t: (0, 0))],
        out_specs=[pl.BlockSpec((R, D), lambda g, t: (g * NT + t, 0)),
                   pl.BlockSpec((SB, GLA_H, GLA_DK, GLA_DV), lambda g, t: (g, 0, 0, 0))],
        out_shape=[jax.ShapeDtypeStruct((B * L, D), BF16 if C % 16 == 0 else F32),
                   jax.ShapeDtypeStruct((B, GLA_H, GLA_DK, GLA_DV), F32)],
        scratch_shapes=[pltpu.VMEM((SB, GLA_H, GLA_DV, GLA_DK), F32),
                        pltpu.VMEM((R, GLA_H * GLA_DK), F32)],
        compiler_params=_cparams(("parallel", "arbitrary")),
        name="gla",
    )(p1, p1, p1, p1, p2, s0, wa, ba, nw)


def _rope128(x, cos, sin, lo):
    rot = jnp.where(lo, pltpu.roll(x, 96, 1), pltpu.roll(x, 32, 1))
    return x * cos + rot * sin


def _q_kernel(mq_ref, anw_ref, w_ref, qnw_ref, cos_ref, sin_ref, q_ref):
    xn = _rms(mq_ref[...], anw_ref[...]).astype(BF16)
    q = _dot(xn, w_ref[...])
    lo = lax.broadcasted_iota(jnp.int32, (1, LANES), 1) < 32
    cos = cos_ref[...]
    sin = sin_ref[...]
    wn = qnw_ref[:, 0:128] * Q_SCALE
    wr = qnw_ref[:, 128:256] * Q_SCALE
    for h in range(MLA_H):
        qn = q[:, h * HEAD_PAD:h * HEAD_PAD + 128]
        qr = q[:, h * HEAD_PAD + 128:(h + 1) * HEAD_PAD]
        ssq = jnp.sum(qn * qn + qr * qr, axis=-1, keepdims=True)
        rinv = lax.rsqrt(ssq / MLA_DQK + EPS)
        qn = qn * rinv * wn
        qr = _rope128(qr * rinv * wr, cos, sin, lo)
        q_ref[:, h * HEAD_PAD:h * HEAD_PAD + 128] = qn.astype(BF16)
        q_ref[:, h * HEAD_PAD + 128:(h + 1) * HEAD_PAD] = qr.astype(BF16)


def _qpath(p1, anw, w_uq, qnw, cos, sin):
    M = p1.shape[0]
    tr = cos.shape[0]
    tm = _pick(math.gcd(M, tr), 512)
    nper = tr // tm
    return pl.pallas_call(
        _q_kernel,
        grid=(M // tm,),
        in_specs=[pl.BlockSpec((tm, Q_LORA), lambda i: (i, 4)),
                  pl.BlockSpec((1, Q_LORA), lambda i: (0, 0)),
                  pl.BlockSpec((Q_LORA, MLA_H * HEAD_PAD), lambda i: (0, 0)),
                  pl.BlockSpec((1, HEAD_PAD), lambda i: (0, 0)),
                  pl.BlockSpec((tm, LANES), lambda i: (i % nper, 0)),
                  pl.BlockSpec((tm, LANES), lambda i: (i % nper, 0))],
        out_specs=pl.BlockSpec((tm, MLA_H * HEAD_PAD), lambda i: (i, 0)),
        out_shape=jax.ShapeDtypeStruct((M, MLA_H * HEAD_PAD), BF16),
        compiler_params=_cparams(("parallel",)),
        name="qpath",
    )(p1, anw, w_uq, qnw, cos, sin)


def _kv_kernel(mkv_ref, tail_ref, anw_ref, wuk_ref, wuv_ref, kw_ref, cos_ref, sin_ref, c_ref, *kv_refs):
    c = _rms(mkv_ref[...], anw_ref[...])
    c_ref[...] = c
    if not kv_refs:
        return
    k_ref, v_ref = kv_refs
    cb = c.astype(BF16)
    kn = _dot(cb, wuk_ref[...])
    v_ref[...] = _dot(cb, wuv_ref[...]).astype(BF16)
    lane = lax.broadcasted_iota(jnp.int32, (1, LANES), 1)
    pe = jnp.where(lane < MLA_ROPE, tail_ref[...], 0.0)
    ssq_pe = jnp.sum(pe * pe, axis=-1, keepdims=True)
    kr = _rope128(pe * kw_ref[:, 128:256], cos_ref[...], sin_ref[...], lane < 32)
    wn = kw_ref[:, 0:128]
    for h in range(MLA_H):
        knh = kn[:, h * 128:(h + 1) * 128]
        rinv = lax.rsqrt((jnp.sum(knh * knh, axis=-1, keepdims=True) + ssq_pe) / MLA_DQK + EPS)
        k_ref[:, h * HEAD_PAD:h * HEAD_PAD + 128] = (knh * rinv * wn).astype(BF16)
        k_ref[:, h * HEAD_PAD + 128:(h + 1) * HEAD_PAD] = (kr * rinv).astype(BF16)


def _kvprep(p1, p2, anw, wuk, wuv, kw, cos, sin, *, with_kv):
    M = p1.shape[0]
    tr = cos.shape[0]
    tm = _pick(math.gcd(M, tr), 512)
    nper = tr // tm
    out_specs = [pl.BlockSpec((tm, KV_LORA), lambda i: (i, 0))]
    out_shape = [jax.ShapeDtypeStruct((M, KV_LORA), F32)]
    if with_kv:
        out_specs += [pl.BlockSpec((tm, MLA_H * HEAD_PAD), lambda i: (i, 0)),
                      pl.BlockSpec((tm, MLA_H * MLA_DV), lambda i: (i, 0))]
        out_shape += [jax.ShapeDtypeStruct((M, MLA_H * HEAD_PAD), BF16),
                      jax.ShapeDtypeStruct((M, MLA_H * MLA_DV), BF16)]
    return pl.pallas_call(
        _kv_kernel,
        grid=(M // tm,),
        in_specs=[pl.BlockSpec((tm, KV_LORA), lambda i: (i, 5)),
                  pl.BlockSpec((tm, TAIL_COLS), lambda i: (i, 0)),
                  pl.BlockSpec((1, KV_LORA), lambda i: (0, 0)),
                  pl.BlockSpec((KV_LORA, MLA_H * MLA_NOPE), lambda i: (0, 0)),
                  pl.BlockSpec((KV_LORA, MLA_H * MLA_DV), lambda i: (0, 0)),
                  pl.BlockSpec((1, HEAD_PAD), lambda i: (0, 0)),
                  pl.BlockSpec((tm, LANES), lambda i: (i % nper, 0)),
                  pl.BlockSpec((tm, LANES), lambda i: (i % nper, 0))],
        out_specs=out_specs,
        out_shape=out_shape,
        compiler_params=_cparams(("parallel",)),
        name="kvprep",
    )(p1, p2, anw, wuk, wuv, kw, cos, sin)


def _attn_kernel(q_ref, k_ref, v_ref, o_ref, s_scr, m_scr, l_scr, acc_scr, *, TQ, TK):
    R = TQ // TK
    HL = TK // LANES
    nfull = pl.program_id(2) * R

    def fold(x, op):
        y = x[:, 0:LANES]
        for g in range(1, HL):
            y = op(y, x[:, g * LANES:(g + 1) * LANES])
        return y

    def kv_rows(j):
        return pl.ds(pl.multiple_of(j * TK, TK), TK)

    m_scr[...] = jnp.full((TQ, LANES), -jnp.inf, F32)

    U = 2 if R % 2 == 0 else 1

    def full_a(jj, carry):
        m = m_scr[...]
        for u in range(U):
            j = jj * U + u
            s = _dot_nt(q_ref[...], k_ref[kv_rows(j), :])
            s_scr[j] = s
            m = jnp.maximum(m, fold(s, jnp.maximum))
        m_scr[...] = m
        return carry

    lax.fori_loop(0, nfull // U, full_a, 0)
    tri = lax.broadcasted_iota(jnp.int32, (TK, TK), 1) <= lax.broadcasted_iota(jnp.int32, (TK, TK), 0)
    for d in range(R):
        r0 = d * TK
        j = nfull + d
        s = _dot_nt(q_ref[r0:, :], k_ref[kv_rows(j), :])
        sd = jnp.where(tri, s[0:TK], -jnp.inf)
        s_scr[j, r0:r0 + TK, :] = sd
        m_scr[r0:r0 + TK, :] = jnp.maximum(m_scr[r0:r0 + TK, :], fold(sd, jnp.maximum))
        if r0 + TK < TQ:
            s_scr[j, r0 + TK:, :] = s[TK:]
            m_scr[r0 + TK:, :] = jnp.maximum(m_scr[r0 + TK:, :], fold(s[TK:], jnp.maximum))

    m_scr[...] = jnp.broadcast_to(jnp.max(m_scr[...], axis=-1, keepdims=True), (TQ, LANES))
    l_scr[...] = jnp.zeros((TQ, LANES), F32)
    acc_scr[...] = jnp.zeros((TQ, MLA_DV), F32)

    def probs(s, m):
        return jnp.concatenate([jnp.exp2(s[:, g * LANES:(g + 1) * LANES] - m) for g in range(HL)], axis=1)

    def full_b(jj, carry):
        ps = [probs(s_scr[jj * U + u], m_scr[...]) for u in range(U)]
        l_scr[...] += functools.reduce(jnp.add, [fold(p, jnp.add) for p in ps])
        pv = jnp.concatenate(ps, axis=1).astype(BF16)
        acc_scr[...] += _dot(pv, v_ref[pl.ds(pl.multiple_of(jj * (U * TK), U * TK), U * TK), :])
        return carry

    lax.fori_loop(0, nfull // U, full_b, 0)
    for d in range(R):
        r0 = d * TK
        j = nfull + d
        p = probs(s_scr[j, r0:, :], m_scr[r0:, :])
        l_scr[r0:, :] += fold(p, jnp.add)
        acc_scr[r0:, :] += _dot(p.astype(BF16), v_ref[kv_rows(j), :])
    o_ref[...] = (acc_scr[...] / jnp.sum(l_scr[...], axis=-1, keepdims=True)).astype(o_ref.dtype)


def _attn(q, k, v, *, B, L):
    TK = 256
    TQ = _pick(L, 1024, TK)
    NQ = L // TQ
    kern = functools.partial(_attn_kernel, TQ=TQ, TK=TK)
    return pl.pallas_call(
        kern,
        grid=(B, MLA_H, NQ),
        in_specs=[pl.BlockSpec((TQ, HEAD_PAD), lambda b, h, i: (b * NQ + i, h)),
                  pl.BlockSpec((L, HEAD_PAD), lambda b, h, i: (b, h)),
                  pl.BlockSpec((L, MLA_DV), lambda b, h, i: (b, h))],
        out_specs=pl.BlockSpec((TQ, MLA_DV), lambda b, h, i: (b * NQ + i, h)),
        out_shape=jax.ShapeDtypeStruct((B * L, MLA_H * MLA_DV), BF16),
        scratch_shapes=[pltpu.VMEM((L // TK, TQ, TK), F32), pltpu.VMEM((TQ, LANES), F32),
                        pltpu.VMEM((TQ, LANES), F32), pltpu.VMEM((TQ, MLA_DV), F32)],
        compiler_params=_cparams(("parallel", "parallel", "arbitrary")),
        name="attn",
    )(q, k, v)


def _qs_kernel(q_ref, wukT_ref, kw_ref, qlat_ref, a_ref, *, DB, DL):
    q = q_ref[...].astype(F32)
    qn = (q[:, 0:128] * kw_ref[:, 0:128]).astype(BF16)
    qlat_ref[...] = _dot(qn, wukT_ref[...]).reshape(DB, 1, DL, KV_LORA)
    qr = q[:, 128:256]
    wr = kw_ref[:, 128:256]
    lo = lax.broadcasted_iota(jnp.int32, (1, LANES), 1) < 32
    a1 = qr * wr
    a2 = jnp.where(lo, pltpu.roll(qr, 96, 1), -pltpu.roll(qr, 32, 1)) * wr
    a_ref[...] = (a1 + pltpu.roll(a2, 64, 1)).reshape(DB, 1, DL, LANES)


def _qs(q_s, wukT, kw, *, DB, DL):
    kern = functools.partial(_qs_kernel, DB=DB, DL=DL)
    return pl.pallas_call(
        kern,
        grid=(MLA_H,),
        in_specs=[pl.BlockSpec((DB * DL, HEAD_PAD), lambda h: (0, h)),
                  pl.BlockSpec((MLA_NOPE, KV_LORA), lambda h: (h, 0)),
                  pl.BlockSpec((1, HEAD_PAD), lambda h: (0, 0))],
        out_specs=[pl.BlockSpec((DB, 1, DL, KV_LORA), lambda h: (0, h, 0, 0)),
                   pl.BlockSpec((DB, 1, DL, LANES), lambda h: (0, h, 0, 0))],
        out_shape=[jax.ShapeDtypeStruct((DB, MLA_H, DL, KV_LORA), F32),
                   jax.ShapeDtypeStruct((DB, MLA_H, DL, LANES), F32)],
        compiler_params=_cparams(("parallel",)),
        name="qs",
    )(q_s, wukT, kw)


def _decode_kernel(pt_ref, lat_hbm, pet_hbm, cst_ref, csn_ref, qlat_ref, a_ref, wukT_ref, cnew_ref, tnew_ref,
                   wuv_ref, y_ref, lat_buf, pet_buf, sem, wall_scr, cb_scr, k2t_scr, m_scr, l_scr, acc_scr,
                   *, PPS, NS, DL, CP, DB, layer):
    b = pl.program_id(0)
    s = pl.program_id(1)
    NR = MLA_H * DL
    CK = CP * PAGE

    def page_copies(bb, ss, slot):
        base = (bb * NS + ss) * PPS
        cps = []
        for i in range(PPS):
            pg = pt_ref[base + i]
            keys = pl.ds(i * PAGE, PAGE)
            cps.append(pltpu.make_async_copy(lat_hbm.at[layer, pg], lat_buf.at[slot, keys, :], sem.at[0, slot]))
            cps.append(pltpu.make_async_copy(pet_hbm.at[layer, pg], pet_buf.at[slot, :, keys], sem.at[1, slot]))
        return cps

    @pl.when((b == 0) & (s == 0))
    def _():
        wall_scr[0:MLA_H * MLA_NOPE, :] = wukT_ref[...]
        for cp in page_copies(0, 0, 0):
            cp.start()

    @pl.when(s == 0)
    def _():
        wall_scr[MLA_H * MLA_NOPE:, :] = qlat_ref[...].reshape(NR, KV_LORA).astype(BF16)
        m_scr[...] = jnp.full((NR, 1), -jnp.inf, F32)
        l_scr[...] = jnp.zeros((NR, 1), F32)
        acc_scr[...] = jnp.zeros((NR, KV_LORA), F32)

    a = a_ref[...].reshape(NR, LANES).astype(BF16)

    def scores(cb, s_rope, ssq_pe):
        n = cb.shape[0]
        big = _dot_nt(wall_scr[...], cb)
        kn = big[0:MLA_H * MLA_NOPE, :].reshape(MLA_H, MLA_NOPE, n)
        rinv = lax.rsqrt((jnp.sum(kn * kn, axis=1) + ssq_pe) / MLA_DQK + EPS)
        sc = big[MLA_H * MLA_NOPE:, :] + s_rope
        return (sc.reshape(MLA_H, DL, n) * rinv[:, None, :]).reshape(NR, n)

    def update(sc, vals):
        m_old = m_scr[...]
        m_new = jnp.maximum(m_old, jnp.max(sc, axis=-1, keepdims=True))
        alpha = jnp.exp2(m_old - m_new)
        p = jnp.exp2(sc - m_new)
        l_scr[...] = l_scr[...] * alpha + jnp.sum(p, axis=-1, keepdims=True)
        acc_scr[...] = acc_scr[...] * alpha + _dot(p.astype(BF16), vals)
        m_scr[...] = m_new

    @pl.when(s < NS)
    def _():
        slot = (b * NS + s) % 2
        for cp in page_copies(b, s, slot):
            cp.wait()
        last = s == NS - 1
        nb = jnp.where(last, b + 1, b)
        ns = jnp.where(last, 0, s + 1)

        @pl.when(nb < DB)
        def _():
            for cp in page_copies(nb, ns, 1 - slot):
                cp.start()

        chunks = []
        for c in range(PPS // CP):
            sq = []
            for i in range(c * CP, (c + 1) * CP):
                keys = slice(i * PAGE, (i + 1) * PAGE)
                cb_scr[keys, :] = lat_buf[slot, keys, :].astype(BF16)
                pet = pet_buf[slot, :, keys]
                k2t_scr[0:MLA_ROPE, keys] = (pet * cst_ref[0:MLA_ROPE, keys]).astype(BF16)
                k2t_scr[MLA_ROPE:, keys] = (pet * cst_ref[MLA_ROPE:, keys]).astype(BF16)
                sq.append(jnp.sum(pet * pet, axis=0, keepdims=True))
            ck = slice(c * CK, (c + 1) * CK)
            s_rope = _dot(a, k2t_scr[:, ck])
            chunks.append((scores(cb_scr[ck, :], s_rope, jnp.concatenate(sq, axis=1)), ck))
        for sc, ck in chunks:
            update(sc, cb_scr[ck, :])

    @pl.when(s == NS)
    def _():
        lane = lax.broadcasted_iota(jnp.int32, (1, LANES), 1)
        pad = lambda x: jnp.concatenate([x, jnp.zeros((PAGE - DL, x.shape[1]), F32)], axis=0)
        cn = pad(cnew_ref[...]).astype(BF16)
        pe = jnp.where(lane < MLA_ROPE, tnew_ref[...], 0.0)
        k2 = pad((pe + pltpu.roll(pe, 64, 1)) * csn_ref[...]).astype(BF16)
        sq = pad(pe * pe)
        sqh = sq.astype(BF16)
        sql = (sq - sqh.astype(F32)).astype(BF16)
        ones = jnp.ones((MLA_H, LANES), BF16)
        sc = scores(cn, _dot_nt(a, k2), _dot_nt(ones, sqh) + _dot_nt(ones, sql))
        qidx = lax.broadcasted_iota(jnp.int32, (MLA_H, DL, PAGE), 1).reshape(NR, PAGE)
        kidx = lax.broadcasted_iota(jnp.int32, (NR, PAGE), 1)
        update(jnp.where(kidx <= qidx, sc, -jnp.inf), cn)
        o = (acc_scr[...] / l_scr[...]).astype(BF16)
        full = _dot(o, wuv_ref[...])
        for h in range(MLA_H):
            y_ref[:, h * MLA_DV:(h + 1) * MLA_DV] = full[h * DL:(h + 1) * DL, h * MLA_DV:(h + 1) * MLA_DV]


def _decode(page_table, cache_lat, cache_pet, layer, cst, csn, qlat, a, wukT, c_new, tail_new, wuv, *, DB, DL, PPS):
    n_pages = page_table.shape[1]
    NS = n_pages // PPS
    TK = PPS * PAGE
    NR = MLA_H * DL
    CP = math.gcd(PPS, 4)

    in_specs = [pl.BlockSpec(memory_space=pl.ANY),
                pl.BlockSpec(memory_space=pl.ANY),
                pl.BlockSpec((LANES, TK), lambda b, s, pt: (0, jnp.minimum(s, NS - 1))),
                pl.BlockSpec((DL, LANES), lambda b, s, pt: (0, 0)),
                pl.BlockSpec((None, MLA_H, DL, KV_LORA), lambda b, s, pt: (b, 0, 0, 0)),
                pl.BlockSpec((None, MLA_H, DL, LANES), lambda b, s, pt: (b, 0, 0, 0)),
                pl.BlockSpec((MLA_H * MLA_NOPE, KV_LORA), lambda b, s, pt: (0, 0)),
                pl.BlockSpec((DL, KV_LORA), lambda b, s, pt: (b, 0)),
                pl.BlockSpec((DL, TAIL_COLS), lambda b, s, pt: (b, 0)),
                pl.BlockSpec((KV_LORA, MLA_H * MLA_DV), lambda b, s, pt: (0, 0))]
    grid_spec = pltpu.PrefetchScalarGridSpec(
        num_scalar_prefetch=1,
        grid=(DB, NS + 1),
        in_specs=in_specs,
        out_specs=pl.BlockSpec((DL, MLA_H * MLA_DV), lambda b, s, pt: (b, 0)),
        scratch_shapes=[pltpu.VMEM((2, TK, KV_LORA), F32),
                        pltpu.VMEM((2, MLA_ROPE, TK), F32),
                        pltpu.SemaphoreType.DMA((2, 2)),
                        pltpu.VMEM((MLA_H * MLA_NOPE + NR, KV_LORA), BF16),
                        pltpu.VMEM((TK, KV_LORA), BF16),
                        pltpu.VMEM((LANES, TK), BF16),
                        pltpu.VMEM((NR, 1), F32),
                        pltpu.VMEM((NR, 1), F32),
                        pltpu.VMEM((NR, KV_LORA), F32)])
    kern = functools.partial(_decode_kernel, PPS=PPS, NS=NS, DL=DL, CP=CP, DB=DB, layer=layer)
    return pl.pallas_call(
        kern,
        grid_spec=grid_spec,
        out_shape=jax.ShapeDtypeStruct((DB * DL, MLA_H * MLA_DV), F32),
        compiler_params=_cparams(("arbitrary", "arbitrary")),
        name="decode",
    )(page_table.reshape(-1), cache_lat, cache_pet, cst, csn, qlat, a, wukT, c_new, tail_new, wuv)


def _merge_kernel(g0_ref, g1_ref, g2_ref, bg_ref, yp_ref, yg_ref, ym_ref, x_ref, w_ref, o_ref):
    def gate(g_ref, i):
        return 1.0 / (1.0 + jnp.exp(-(g_ref[...].astype(F32) + bg_ref[i:i + 1, :])))

    merged = gate(g0_ref, 0) * yp_ref[...] + gate(g1_ref, 1) * yg_ref[...] + gate(g2_ref, 2) * ym_ref[...]
    o_ref[...] = x_ref[...] + _dot(merged.astype(BF16), w_ref[...])


def _merge(p1, bg, yp, yg, ym, x, w_out):
    M = x.shape[0]
    tm = _pick(M, 512)
    row = lambda i: (i, 0)
    return pl.pallas_call(
        _merge_kernel,
        grid=(M // tm,),
        in_specs=[pl.BlockSpec((tm, D), lambda i: (i, 3)),
                  pl.BlockSpec((tm, D), lambda i: (i, 4)),
                  pl.BlockSpec((tm, D), lambda i: (i, 5)),
                  pl.BlockSpec((3, D), lambda i: (0, 0)),
                  pl.BlockSpec((tm, D), row), pl.BlockSpec((tm, D), row), pl.BlockSpec((tm, D), row),
                  pl.BlockSpec((tm, D), row),
                  pl.BlockSpec((D, D), lambda i: (0, 0))],
        out_specs=pl.BlockSpec((tm, D), row),
        out_shape=jax.ShapeDtypeStruct((M, D), F32),
        compiler_params=_cparams(("parallel",)),
        name="merge",
    )(p1, p1, p1, bg, yp, yg, ym, x, w_out)


def _mlp_kernel(x_ref, nw_ref, wu_ref, wd_ref, o_ref, h_scr, acc_scr):
    f = pl.program_id(1)

    @pl.when(f == 0)
    def _():
        h_scr[...] = _rms(x_ref[...], nw_ref[...]).astype(BF16)
        acc_scr[...] = jnp.zeros_like(acc_scr)

    a = jnp.maximum(_dot(h_scr[...], wu_ref[...]), 0.0)
    acc_scr[...] += _dot((a * a).astype(BF16), wd_ref[...])

    @pl.when(f == pl.num_programs(1) - 1)
    def _():
        o_ref[...] = x_ref[...] + acc_scr[...]


def _mlp(x, nw, w_up, w_down):
    M = x.shape[0]
    tm = _pick(M, 1024)
    tf = 512
    return pl.pallas_call(
        _mlp_kernel,
        grid=(M // tm, D_FF // tf),
        in_specs=[pl.BlockSpec((tm, D), lambda i, f: (i, 0)),
                  pl.BlockSpec((1, D), lambda i, f: (0, 0)),
                  pl.BlockSpec((D, tf), lambda i, f: (0, f)),
                  pl.BlockSpec((tf, D), lambda i, f: (f, 0))],
        out_specs=pl.BlockSpec((tm, D), lambda i, f: (i, 0)),
        out_shape=jax.ShapeDtypeStruct((M, D), F32),
        scratch_shapes=[pltpu.VMEM((tm, D), BF16), pltpu.VMEM((tm, D), F32)],
        compiler_params=_cparams(("parallel", "arbitrary")),
        name="mlp",
    )(x, nw, w_up, w_down)


def _rope_angles(pos):
    inv = 1.0 / (ROPE_THETA ** (jnp.arange(0, MLA_ROPE, 2, dtype=F32) / MLA_ROPE))
    return pos.astype(F32)[:, None] * inv[None, :]


def _token_tables(pos):
    ang = _rope_angles(pos)
    c, s = jnp.cos(ang), jnp.sin(ang)
    z = jnp.zeros((pos.shape[0], LANES - MLA_ROPE), F32)
    return jnp.concatenate([c, c, z], axis=1), jnp.concatenate([-s, s, z], axis=1)


def _key_table(pos):
    ang = _rope_angles(pos)
    c, s = jnp.cos(ang), jnp.sin(ang)
    return jnp.concatenate([c, c, s, s], axis=1)


def _pad_heads(w):
    lead = w.shape[:-1]
    w = w.reshape(lead + (MLA_H, MLA_DQK))
    w = jnp.concatenate([w, jnp.zeros(lead + (MLA_H, HEAD_PAD - MLA_DQK), w.dtype)], axis=-1)
    return w.reshape(lead + (MLA_H * HEAD_PAD,))


def kernel(x_prompt, x_sample, state_pool, state_gla, cache_kv_latent, cache_k_rope, page_table,
           norm1_w, w_in, b_gate, w_pool, pool_scale, w_gla_a2, b_gla_a, gla_norm_w,
           q_a_norm_w, kv_a_norm_w, w_uq, q_norm_w, w_uk, w_uv, k_norm_w, w_out,
           norm2_w, w_up, w_down):
    return _forward(x_prompt, x_sample, state_pool, state_gla, cache_kv_latent, cache_k_rope, page_table,
                    norm1_w, w_in, b_gate, w_pool, pool_scale, w_gla_a2, b_gla_a, gla_norm_w,
                    q_a_norm_w, kv_a_norm_w, w_uq, q_norm_w, w_uk, w_uv, k_norm_w, w_out,
                    norm2_w, w_up, w_down, pages_per_step=16)


def _forward(x_prompt, x_sample, state_pool, state_gla, cache_kv_latent, cache_k_rope, page_table,
             norm1_w, w_in, b_gate, w_pool, pool_scale, w_gla_a2, b_gla_a, gla_norm_w,
             q_a_norm_w, kv_a_norm_w, w_uq, q_norm_w, w_uk, w_uv, k_norm_w, w_out,
             norm2_w, w_up, w_down, *, pages_per_step):
    B, L, _ = x_prompt.shape
    DB, DL, _ = x_sample.shape
    depth = w_in.shape[0]
    n_pages = page_table.shape[1]
    past = n_pages * PAGE
    PPS = pages_per_step
    assert n_pages % PPS == 0

    o = [0]
    for n in (1024, 512, 512, 1024, GLA_RANK, 1024, Q_LORA, KV_LORA, MLA_ROPE, 3 * D):
        o.append(o[-1] + n)
    seg = lambda i: w_in[:, :, o[i]:o[i + 1]]
    w_main = jnp.concatenate([seg(0), seg(6), seg(7), seg(1), seg(2), seg(3), seg(5), seg(9)], axis=-1).astype(BF16)
    w_tail = jnp.concatenate([seg(8), seg(4), jnp.zeros((depth, D, TAIL_COLS - MLA_ROPE - GLA_RANK), F32)],
                             axis=-1).astype(BF16)
    wa_pad = jnp.zeros((depth, TAIL_COLS, GLA_H * GLA_DK), F32).at[:, MLA_ROPE:MLA_ROPE + GLA_RANK, :].set(w_gla_a2)
    wa_pad = wa_pad.astype(BF16)
    w_pool_b = w_pool.astype(BF16)
    w_uq_p = _pad_heads(w_uq).astype(BF16)
    qnw_p = jnp.concatenate([q_norm_w, jnp.zeros((depth, HEAD_PAD - MLA_DQK), F32)], axis=-1)[:, None, :]
    knw_p = jnp.concatenate([k_norm_w, jnp.zeros((depth, HEAD_PAD - MLA_DQK), F32)], axis=-1)[:, None, :]
    w_uk_b = w_uk.astype(BF16)
    w_ukT_b = jnp.swapaxes(w_uk, 1, 2).astype(BF16)
    w_uv_b = w_uv.astype(BF16)
    w_out_b = w_out.astype(BF16)
    w_up_b = w_up.astype(BF16)
    w_down_b = w_down.astype(BF16)

    cos_p, sin_p = _token_tables(jnp.arange(L))
    reps = math.gcd(DB, 512 // DL)
    cos_s, sin_s = _token_tables(jnp.tile(past + jnp.arange(DL), reps))
    cs_past_t = _key_table(jnp.arange(past)).T
    cs_new = _key_table(past + jnp.arange(DL))
    cache_pet = jnp.swapaxes(cache_k_rope, 2, 3)

    xp = x_prompt.reshape(B * L, D)
    xs = x_sample.reshape(DB * DL, D)
    zero_prefix = jnp.zeros((B, POOL_STATE, D), F32)
    zero_state = jnp.zeros((B, GLA_H, GLA_DK, GLA_DV), F32)

    outs = [[] for _ in range(8)]
    for l in range(depth):
        r1 = lambda w: w[l][None, :]
        p1, pb, p2 = _proj(xp, r1(norm1_w), w_main[l], w_tail[l])
        yp, pool_p = _pool(p1, zero_prefix, w_pool_b[l], r1(pool_scale), B=B, L=L, start_pos=0)
        yg, gla_p = _gla(pb, p2, zero_state, wa_pad[l], r1(b_gla_a), r1(gla_norm_w), B=B, L=L)
        q = _qpath(p1, r1(q_a_norm_w), w_uq_p[l], qnw_p[l], cos_p, sin_p)
        c_p, k, v = _kvprep(p1, p2, r1(kv_a_norm_w), w_uk_b[l], w_uv_b[l], knw_p[l], cos_p, sin_p, with_kv=True)
        ym = _attn(q, k, v, B=B, L=L)
        xp = _merge(pb, b_gate[l], yp, yg, ym, xp, w_out_b[l])
        xp = _mlp(xp, r1(norm2_w), w_up_b[l], w_down_b[l])
        pe_p = p2[:, :MLA_ROPE]
        s1, sb, s2 = _proj(xs, r1(norm1_w), w_main[l], w_tail[l])
        yp, pool_s = _pool(s1, state_pool[l], w_pool_b[l], r1(pool_scale), B=DB, L=DL, start_pos=past)
        yg, gla_s = _gla(sb, s2, state_gla[l], wa_pad[l], r1(b_gla_a), r1(gla_norm_w), B=DB, L=DL)
        q = _qpath(s1, r1(q_a_norm_w), w_uq_p[l], qnw_p[l], cos_s, sin_s)
        (c_s,) = _kvprep(s1, s2, r1(kv_a_norm_w), w_uk_b[l], w_uv_b[l], knw_p[l], cos_s, sin_s, with_kv=False)
        qlat, a = _qs(q, w_ukT_b[l], knw_p[l], DB=DB, DL=DL)
        ym = _decode(page_table, cache_kv_latent, cache_pet, l, cs_past_t, cs_new, qlat, a, w_ukT_b[l], c_s, s2,
                     w_uv_b[l], DB=DB, DL=DL, PPS=PPS)
        xs = _merge(sb, b_gate[l], yp, yg, ym, xs, w_out_b[l])
        xs = _mlp(xs, r1(norm2_w), w_up_b[l], w_down_b[l])
        pe_s = s2[:, :MLA_ROPE]
        for lst, val in zip(outs, (pool_p, pool_s, gla_p, gla_s,
                                   c_p.reshape(B, L, KV_LORA), c_s.reshape(DB, DL, KV_LORA),
                                   pe_p.reshape(B, L, MLA_ROPE), pe_s.reshape(DB, DL, MLA_ROPE))):
            lst.append(val)
    return (xp.reshape(B, L, D), xs.reshape(DB, DL, D)) + tuple(jnp.stack(v) for v in outs)
```
